```python
import jax, jax.numpy as jnp
from jax import lax
import numpy as np

D_MODEL = 1024
BATCH = 2
SEQ = 8192
DEPTH = 2

GRID_W = 64
CTX_LEN = 256
D_MIX = D_MODEL
EPS = 1e-6
LB_FLOOR = 1e-30
NEG_BIG = -1e4
A_HEADS = 4
A_HD = 64
A_WIDTH = A_HEADS * A_HD
CHUNK = 64
B_GROUPS = 4
B_WIDTH = D_MIX // 4
B_GW = B_WIDTH // B_GROUPS
POOL_WINDOWS = (2, 4, 8, 16)
C_HEADS = 8
C_KV_HEADS = 2
C_HD = 64
C_WIDTH = C_HEADS * C_HD
C_KV_WIDTH = C_KV_HEADS * C_HD
Q_BLOCK = 128
ROPE_THETA = 10000.0
N_EXPERTS = 16
N_GROUPS = 4
E_PER_GROUP = N_EXPERTS // N_GROUPS
TOP_K = 2
D_EXPERT = D_MODEL // 2
IN_SIZES = (A_WIDTH, A_WIDTH, A_WIDTH, A_WIDTH, A_WIDTH, B_WIDTH, C_WIDTH, C_KV_WIDTH, C_KV_WIDTH)
IN_COLS = 5 * A_WIDTH + B_WIDTH + C_WIDTH + 2 * C_KV_WIDTH

kernel_name = "hybrid_hgrn2_pool_gqa_grouped_moe_dit"


def rms_norm(x, g):
    xf = x.astype(jnp.float32)
    y = xf * lax.rsqrt(jnp.mean(xf * xf, axis=-1, keepdims=True) + EPS)
    return (y * g.astype(jnp.float32)).astype(x.dtype)


def modulate(h, shift, scale):
    return h * (1 + scale) + shift


def split_in(a):
    outs, off = [], 0
    for s in IN_SIZES:
        outs.append(a[..., off:off + s])
        off += s
    return outs


def axial_rope_tables(T):
    rows_n = T // GRID_W
    row = jnp.repeat(jnp.arange(rows_n), GRID_W).astype(jnp.float32)
    col = jnp.tile(jnp.arange(GRID_W), rows_n).astype(jnp.float32)
    n_freq = C_HD // 4
    inv = ROPE_THETA ** (-jnp.arange(n_freq, dtype=jnp.float32) / n_freq)
    ang = jnp.concatenate([row[:, None] * inv, col[:, None] * inv], axis=-1)
    return jnp.cos(ang), jnp.sin(ang)


def apply_rope(x, cos, sin):
    xf = x.astype(jnp.float32).reshape(*x.shape[:-1], C_HD // 2, 2)
    x1, x2 = xf[..., 0], xf[..., 1]
    cs, sn = cos[None, :, None, :], sin[None, :, None, :]
    out = jnp.stack([x1 * cs - x2 * sn, x1 * sn + x2 * cs], axis=-1)
    return out.reshape(x.shape).astype(x.dtype)


def to_heads(a):
    B, T, _ = a.shape
    return a.astype(jnp.float32).reshape(B, T, A_HEADS, A_HD).transpose(0, 2, 1, 3)


def log_forget(z, lb):
    lbf = jnp.maximum(lb, LB_FLOOR)
    return jnp.logaddexp(jnp.log(lbf), jnp.log1p(-lbf) + jax.nn.log_sigmoid(z))


def gated_chunk_scan(q, k, v, logf, s0):
    B, H, T, dk = q.shape
    n = T // CHUNK

    def chunks(a):
        return a.reshape(B, H, n, CHUNK, a.shape[-1]).transpose(2, 0, 1, 3, 4)

    causal_in_chunk = jnp.tril(jnp.ones((CHUNK, CHUNK), dtype=bool))[:, :, None]
    causal_f = causal_in_chunk.astype(jnp.float32)

    def step(S, inp):
        qc, kc, vc, lc = inp
        b = jnp.cumsum(lc, axis=2)
        o_inter = jnp.einsum('bhtk,bhkv->bhtv', qc * jnp.exp(b), S)
        diff = b[:, :, :, None, :] - b[:, :, None, :, :]
        decay = jnp.exp(jnp.where(causal_in_chunk, diff, 0.0)) * causal_f
        att = jnp.einsum('bhtk,bhtsk,bhsk->bhts', qc, decay, kc)
        o = o_inter + jnp.einsum('bhts,bhsv->bhtv', att, vc)
        b_last = b[:, :, -1:, :]
        S_new = jnp.exp(b_last[:, :, 0, :])[..., None] * S + jnp.einsum(
            'bhsk,bhsv->bhkv', kc * jnp.exp(b_last - b), vc)
        return S_new, o

    S, o = lax.scan(step, s0, (chunks(q), chunks(k), chunks(v), chunks(logf)))
    return o.transpose(1, 2, 0, 3, 4).reshape(B, H, T, v.shape[-1]), S


def flip_seq(a, rev):
    return a[:, :, ::-1] if rev else a


def hgrn2_mixer(lat, ctx, lb, norm_g, need_ctx):
    def prep(qz, ffz, fbz, iz):
        q = jax.nn.silu(to_heads(qz))
        v = to_heads(iz)
        logfs = [log_forget(to_heads(fz), lb[d].reshape(A_HEADS, 1, A_HD))
                 for d, fz in enumerate((ffz, fbz))]
        return q, v, logfs

    q, v, logfs = prep(*lat[:4])
    qc, vc, logfcs = prep(*ctx[:4])
    B, H, L, _ = qc.shape
    o, oc = 0.0, 0.0
    for d, rev in enumerate((False, True)):
        lf, lfc = logfs[d], logfcs[d]
        s0 = jnp.zeros((B, H, A_HD, A_HD), jnp.float32)
        oc_d, s_ctx = gated_chunk_scan(flip_seq(qc, rev), flip_seq(-jnp.expm1(lfc), rev),
                                       flip_seq(vc, rev), flip_seq(lfc, rev), s0)
        o_d, _ = gated_chunk_scan(flip_seq(q, rev), flip_seq(-jnp.expm1(lf), rev),
                                  flip_seq(v, rev), flip_seq(lf, rev), s_ctx)
        o = o + flip_seq(o_d, rev)
        oc = oc + flip_seq(oc_d, rev)

    def readout(o_, gz):
        Bq, T, _ = gz.shape
        g = jax.nn.silu(gz.astype(jnp.float32)).reshape(Bq, T, A_HEADS, A_HD)
        return (rms_norm(o_.transpose(0, 2, 1, 3), norm_g) * g).reshape(Bq, T, A_WIDTH).astype(gz.dtype)

    return readout(o, lat[4]), (readout(oc, ctx[4]) if need_ctx else None)


def pool_mixer(h, w, scale):
    B, T, _ = h.shape
    hf = h.astype(jnp.float32).reshape(B, T, B_GROUPS, B_GW)
    P = jnp.concatenate([jnp.zeros((B, 1, B_GROUPS, B_GW), jnp.float32), jnp.cumsum(hf, axis=1)], axis=1)
    t = jnp.arange(T)
    outs = []
    for gi, win in enumerate(POOL_WINDOWS):
        lo = jnp.maximum(t - win // 2, 0)
        hi = jnp.minimum(t + win // 2, T)
        cnt = (hi - lo).astype(jnp.float32)
        mean = (P[:, hi, gi] - P[:, lo, gi]) / cnt[None, :, None]
        outs.append(mean - hf[:, :, gi])
    d = jnp.stack(outs, axis=2)
    y = jnp.einsum('btgc,gce->btge', d, w.astype(jnp.float32)).reshape(B, T, B_WIDTH)
    return (y * scale.astype(jnp.float32)).astype(h.dtype)


def attend(qg, k, v):
    s = jnp.einsum('bhgqd,bhsd->bhgqs', qg, k).astype(jnp.float32) * (C_HD ** -0.5)
    p = jax.nn.softmax(s, axis=-1).astype(v.dtype)
    return jnp.einsum('bhgqs,bhsd->bhgqd', p, v)


def gqa_mixer(q, k, v, qc, kc, vc, qn, kn, cos, sin, need_ctx):
    B, T, _ = q.shape
    L = qc.shape[1]
    G = C_HEADS // C_KV_HEADS

    def heads(a, h):
        return a.reshape(a.shape[0], a.shape[1], h, C_HD)

    q = apply_rope(rms_norm(heads(q, C_HEADS), qn), cos, sin)
    k = apply_rope(rms_norm(heads(k, C_KV_HEADS), kn), cos, sin)
    v = heads(v, C_KV_HEADS)
    kc = rms_norm(heads(kc, C_KV_HEADS), kn)
    vc = heads(vc, C_KV_HEADS)
    keys = jnp.concatenate([kc, k], axis=1).transpose(0, 2, 1, 3)
    vals = jnp.concatenate([vc, v], axis=1).transpose(0, 2, 1, 3)
    nb = T // Q_BLOCK
    qb = q.reshape(B, nb, Q_BLOCK, C_KV_HEADS, G, C_HD).transpose(1, 0, 3, 4, 2, 5)
    ob = lax.map(lambda blk: attend(blk, keys, vals), qb)
    o = ob.transpose(1, 0, 4, 2, 3, 5).reshape(B, T, C_WIDTH)
    oc = None
    if need_ctx:
        qcg = rms_norm(heads(qc, C_HEADS), qn).reshape(B, L, C_KV_HEADS, G, C_HD).transpose(0, 2, 3, 1, 4)
        oc = attend(qcg, kc.transpose(0, 2, 1, 3), vc.transpose(0, 2, 1, 3))
        oc = oc.transpose(0, 3, 1, 2, 4).reshape(B, L, C_WIDTH)
    return o, oc


def grouped_moe(h, router_w, router_b, wg, wu, wd):
    N = h.shape[0]
    scores = jax.nn.sigmoid((h @ router_w).astype(jnp.float32))
    sel = scores + router_b.astype(jnp.float32)
    grp_score = lax.top_k(sel.reshape(N, N_GROUPS, E_PER_GROUP), 2)[0].sum(-1)
    best = jnp.argmax(grp_score, axis=-1)
    in_grp = (jnp.arange(N_EXPERTS) // E_PER_GROUP)[None, :] == best[:, None]
    _, idx = lax.top_k(jnp.where(in_grp, sel, NEG_BIG), TOP_K)
    w = jnp.take_along_axis(scores, idx, axis=-1)
    w = w / jnp.sum(w, axis=-1, keepdims=True)
    combine = jnp.sum(jax.nn.one_hot(idx, N_EXPERTS, dtype=jnp.float32) * w[..., None], axis=1)
    y = jnp.zeros((N, h.shape[1]), jnp.float32)
    for e in range(N_EXPERTS):
        a = jax.nn.silu(h @ wg[e]) * (h @ wu[e])
        y = y + combine[:, e:e + 1] * (a @ wd[e]).astype(jnp.float32)
    return y.astype(h.dtype)


def setup_inputs(seed: int = 0) -> dict:
    key = jax.random.key(seed)
    ks = jax.random.split(key, 21)

    def nrm(k, shape, s):
        return jax.random.normal(k, shape, jnp.float32) * s

    return {
        "x": nrm(ks[0], (BATCH, SEQ, D_MODEL), 1.0),
        "c": nrm(ks[1], (BATCH, D_MODEL), 1.0),
        "ctx": nrm(ks[2], (BATCH, CTX_LEN, D_MODEL), 1.0),
        "c_ctx": nrm(ks[3], (D_MODEL,), 1.0),
        "w_mod": nrm(ks[4], (DEPTH, D_MODEL, 6 * D_MODEL), 0.5 * D_MODEL ** -0.5),
        "b_mod": nrm(ks[5], (DEPTH, 6 * D_MODEL), 0.02),
        "norm1_g": 1.0 + nrm(ks[6], (DEPTH, D_MODEL), 0.05),
        "norm2_g": 1.0 + nrm(ks[7], (DEPTH, D_MODEL), 0.05),
        "w_in": nrm(ks[8], (DEPTH, D_MODEL, IN_COLS), D_MODEL ** -0.5),
        "w_out": nrm(ks[9], (DEPTH, D_MIX, D_MODEL), D_MIX ** -0.5),
        "hgrn_lb": nrm(ks[10], (DEPTH, 2, A_WIDTH), 1.0),
        "hgrn_norm_g": 1.0 + nrm(ks[11], (DEPTH, A_HD), 0.05),
        "pool_w": nrm(ks[12], (DEPTH, B_GROUPS, B_GW, B_GW), B_GW ** -0.5),
        "pool_scale": 1.0 + nrm(ks[13], (DEPTH, B_WIDTH), 0.05),
        "q_norm_g": 1.0 + nrm(ks[14], (DEPTH, C_HD), 0.05),
        "k_norm_g": 1.0 + nrm(ks[15], (DEPTH, C_HD), 0.05),
        "router_w": nrm(ks[16], (D_MODEL, N_EXPERTS), D_MODEL ** -0.5),
        "router_b": nrm(ks[17], (N_EXPERTS,), 0.01),
        "moe_w_gate": nrm(ks[18], (DEPTH, N_EXPERTS, D_MODEL, D_EXPERT), D_MODEL ** -0.5),
        "moe_w_up": nrm(ks[19], (DEPTH, N_EXPERTS, D_MODEL, D_EXPERT), D_MODEL ** -0.5),
        "moe_w_down": nrm(ks[20], (DEPTH, N_EXPERTS, D_EXPERT, D_MODEL), D_EXPERT ** -0.5),
    }


def reference(x, c, ctx, c_ctx, w_mod, b_mod, norm1_g, norm2_g, w_in, w_out, hgrn_lb, hgrn_norm_g,
              pool_w, pool_scale, q_norm_g, k_norm_g, router_w, router_b, moe_w_gate, moe_w_up, moe_w_down):
    B, T, D = x.shape
    L = ctx.shape[1]
    cos, sin = axial_rope_tables(T)
    c_act = jax.nn.silu(c)
    cc_act = jax.nn.silu(c_ctx)
    sm = jax.nn.softmax(hgrn_lb.astype(jnp.float32), axis=0)
    lower = jnp.cumsum(sm, axis=0) - sm[0:1]

    xl, xc = x, ctx
    for l in range(DEPTH):
        need_ctx = l < DEPTH - 1
        mod = c_act @ w_mod[l] + b_mod[l]
        sh1, sc1, ga1, sh2, sc2, ga2 = jnp.split(mod[:, None, :], 6, axis=-1)
        sh1c, sc1c, ga1c, sh2c, sc2c, ga2c = jnp.split(cc_act @ w_mod[l] + b_mod[l], 6)

        h = modulate(rms_norm(xl, norm1_g[l]), sh1, sc1)
        hc = modulate(rms_norm(xc, norm1_g[l]), sh1c, sc1c)
        aq, aff, afb, ai, ag, bp, cq, ck, cv = split_in(h @ w_in[l])
        aqc, affc, afbc, aic, agc, bpc, cqc, ckc, cvc = split_in(hc @ w_in[l])

        oa, oac = hgrn2_mixer((aq, aff, afb, ai, ag), (aqc, affc, afbc, aic, agc),
                              lower[l], hgrn_norm_g[l], need_ctx)
        ob = pool_mixer(bp, pool_w[l], pool_scale[l])
        oc, occ = gqa_mixer(cq, ck, cv, cqc, ckc, cvc, q_norm_g[l], k_norm_g[l], cos, sin, need_ctx)
        xl = xl + ga1 * (jnp.concatenate([oa, ob, oc], axis=-1) @ w_out[l])

        h2 = modulate(rms_norm(xl, norm2_g[l]), sh2, sc2)
        if need_ctx:
            obc = pool_mixer(bpc, pool_w[l], pool_scale[l])
            xc = xc + ga1c * (jnp.concatenate([oac, obc, occ], axis=-1) @ w_out[l])
            h2c = modulate(rms_norm(xc, norm2_g[l]), sh2c, sc2c)
            m = grouped_moe(jnp.concatenate([h2.reshape(B * T, D), h2c.reshape(B * L, D)], axis=0),
                            router_w, router_b, moe_w_gate[l], moe_w_up[l], moe_w_down[l])
            xl = xl + ga2 * m[:B * T].reshape(B, T, D)
            xc = xc + ga2c * m[B * T:].reshape(B, L, D)
        else:
            m = grouped_moe(h2.reshape(B * T, D), router_w, router_b,
                            moe_w_gate[l], moe_w_up[l], moe_w_down[l])
            xl = xl + ga2 * m.reshape(B, T, D)
    return xl
```

```python
import functools
import math

import jax
import jax.numpy as jnp
import numpy as np
from jax import lax
from jax.experimental import pallas as pl
from jax.experimental.pallas import tpu as pltpu

F32 = jnp.float32
BF16 = jnp.bfloat16

EPS = 1e-6
LB_FLOOR = 1e-30
NEG_BIG = -1e4
GRID_W = 64
ROPE_THETA = 10000.0
HD = 64
A_HEADS = 4
A_WIDTH = A_HEADS * HD
B_GROUPS = 4
POOL_WINDOWS = (2, 4, 8, 16)
POOL_HALO = 8
C_HEADS = 8
C_KV_HEADS = 2
C_GROUP = C_HEADS // C_KV_HEADS
N_EXPERTS = 16
N_GROUPS = 4
E_PER_GROUP = N_EXPERTS // N_GROUPS
MASK_NEG = -1e30

TOKEN_TILE = 256
HGRN_CHUNK = 128
ATTN_TQ = 256
MOE_BM = 256
VMEM_LIMIT = 56 * 1024 * 1024


def _cparams(sem):
    return pltpu.CompilerParams(dimension_semantics=sem, vmem_limit_bytes=VMEM_LIMIT)


def _dot(a, b):
    return jnp.dot(a, b, preferred_element_type=F32)


def _dot_nt(a, b):
    return lax.dot_general(a, b, (((1,), (1,)), ((), ())), preferred_element_type=F32)


def _dot_tn(a, b):
    return lax.dot_general(a, b, (((0,), (0,)), ((), ())), preferred_element_type=F32)


def _split_bf16(x):
    hi = x.astype(BF16)
    lo = (x - hi.astype(F32)).astype(BF16)
    return hi, lo


def _seg_mean_sq(x, bd):
    hi, lo = _split_bf16(x * x)
    return (_dot(hi, bd) + _dot(lo, bd)) * (1.0 / HD)


def _swap_halves(x):
    w = x.shape[-1]
    lane = lax.broadcasted_iota(jnp.int32, x.shape, x.ndim - 1)
    up = pltpu.roll(x, w - HD // 2, axis=x.ndim - 1)
    dn = pltpu.roll(x, HD // 2, axis=x.ndim - 1)
    return jnp.where((lane & (HD // 2)) == 0, up, dn)


def _mod_kernel(c_ref, w_ref, b_ref, o_ref):
    c = c_ref[...]
    a = c * jax.nn.sigmoid(c)
    hi, lo = _split_bf16(a)
    whi, wlo = _split_bf16(w_ref[0])
    o_ref[0] = _dot(hi, whi) + _dot(lo, whi) + _dot(hi, wlo) + b_ref[0]


def _mod_call(cond, w_mod, b_mod):
    depth, d, n6 = w_mod.shape
    tn = 1536
    return pl.pallas_call(
        _mod_kernel,
        out_shape=jax.ShapeDtypeStruct((depth, 8, n6), F32),
        grid=(depth, n6 // tn),
        in_specs=[
            pl.BlockSpec((8, d), lambda l, j: (0, 0)),
            pl.BlockSpec((1, d, tn), lambda l, j: (l, 0, j)),
            pl.BlockSpec((1, 1, tn), lambda l, j: (l, 0, j)),
        ],
        out_specs=pl.BlockSpec((1, 8, tn), lambda l, j: (l, 0, j)),
        compiler_params=_cparams(("arbitrary", "arbitrary")),
        name="adaln_mod",
    )(cond, w_mod, b_mod.reshape(depth, 1, n6))


def _in_kernel(*refs, has_moe):
    if has_moe:
        (x_ref, m_ref, modp_ref, mod_ref, g_ref, w_ref, cos_ref, sin_ref, gq_ref, gk_ref,
         bdq_ref, bdk_ref, xo_ref, hg_ref, pool_ref, q_ref, k_ref, v_ref) = refs
        x = x_ref[0] + modp_ref[0, 0, 5:6, :] * m_ref[0].astype(F32)
        xo_ref[0] = x
    else:
        (x_ref, mod_ref, g_ref, w_ref, cos_ref, sin_ref, gq_ref, gk_ref,
         bdq_ref, bdk_ref, hg_ref, pool_ref, q_ref, k_ref, v_ref) = refs
        x = x_ref[0]
    ms = jnp.mean(x * x, axis=-1, keepdims=True)
    h = x * lax.rsqrt(ms + EPS) * g_ref[...]
    h = h * (1.0 + mod_ref[0, 0, 1:2, :]) + mod_ref[0, 0, 0:1, :]
    y = _dot(h.astype(BF16), w_ref[...])
    na = 5 * A_WIDTH
    hg_ref[0] = y[:, :na]
    pool_ref[0] = y[:, na:na + 256]
    q0 = na + 256
    qw = C_HEADS * HD
    kw = C_KV_HEADS * HD
    cos = cos_ref[...]
    sin = sin_ref[...]
    q = y[:, q0:q0 + qw]
    qn = q * lax.rsqrt(_seg_mean_sq(q, bdq_ref[...]) + EPS) * gq_ref[...]
    cos_q = jnp.concatenate([cos] * (qw // 128), axis=1)
    sin_q = jnp.concatenate([sin] * (qw // 128), axis=1)
    qr = (qn * cos_q + _swap_halves(qn) * sin_q) * (HD ** -0.5)
    for hh in range(C_HEADS):
        q_ref[0, hh] = qr[:, hh * HD:(hh + 1) * HD].astype(BF16)
    k = y[:, q0 + qw:q0 + qw + kw]
    kn = k * lax.rsqrt(_seg_mean_sq(k, bdk_ref[...]) + EPS) * gk_ref[...]
    kr = kn * cos + _swap_halves(kn) * sin
    v = y[:, q0 + qw + kw:q0 + qw + 2 * kw]
    for hh in range(C_KV_HEADS):
        k_ref[0, hh] = kr[:, hh * HD:(hh + 1) * HD].astype(BF16)
        v_ref[0, hh] = v[:, hh * HD:(hh + 1) * HD].astype(BF16)


def _in_call(xs, m, mods_prev, mods, g1, w_in, cos_t, sin_t, gq, gk, bdq, bdk, ctx_len):
    bsz, s, d = xs.shape
    tm = TOKEN_TILE
    nctx = ctx_len // tm
    ncols = w_in.shape[1]
    has_moe = m is not None

    def tok(b, j):
        return (b, j, 0)

    def modmap(b, j):
        return (0, jnp.where(j < nctx, bsz, b), 0, 0)

    const2 = lambda b, j: (0, 0)
    in_specs = [pl.BlockSpec((1, tm, d), tok)]
    args = [xs]
    if has_moe:
        in_specs += [pl.BlockSpec((1, tm, d), tok), pl.BlockSpec((1, 1, 8, d), modmap)]
        args += [m, mods_prev]
    in_specs += [
        pl.BlockSpec((1, 1, 8, d), modmap),
        pl.BlockSpec((1, d), const2),
        pl.BlockSpec((d, ncols), const2),
        pl.BlockSpec((tm, 128), lambda b, j: (j, 0)),
        pl.BlockSpec((tm, 128), lambda b, j: (j, 0)),
        pl.BlockSpec((1, C_HEADS * HD), const2),
        pl.BlockSpec((1, C_KV_HEADS * HD), const2),
        pl.BlockSpec((C_HEADS * HD, C_HEADS * HD), const2),
        pl.BlockSpec((C_KV_HEADS * HD, C_KV_HEADS * HD), const2),
    ]
    args += [mods, g1, w_in, cos_t, sin_t, gq, gk, bdq, bdk]
    out_shape = []
    out_specs = []
    if has_moe:
        out_shape.append(jax.ShapeDtypeStruct((bsz, s, d), F32))
        out_specs.append(pl.BlockSpec((1, tm, d), tok))
    out_shape += [
        jax.ShapeDtypeStruct((bsz, s, 5 * A_WIDTH), F32),
        jax.ShapeDtypeStruct((bsz, s, 256), F32),
        jax.ShapeDtypeStruct((bsz, C_HEADS, s, HD), BF16),
        jax.ShapeDtypeStruct((bsz, C_KV_HEADS, s, HD), BF16),
        jax.ShapeDtypeStruct((bsz, C_KV_HEADS, s, HD), BF16),
    ]
    out_specs += [
        pl.BlockSpec((1, tm, 5 * A_WIDTH), tok),
        pl.BlockSpec((1, tm, 256), tok),
        pl.BlockSpec((1, C_HEADS, tm, HD), lambda b, j: (b, 0, j, 0)),
        pl.BlockSpec((1, C_KV_HEADS, tm, HD), lambda b, j: (b, 0, j, 0)),
        pl.BlockSpec((1, C_KV_HEADS, tm, HD), lambda b, j: (b, 0, j, 0)),
    ]
    return pl.pallas_call(
        functools.partial(_in_kernel, has_moe=has_moe),
        out_shape=out_shape,
        grid=(bsz, s // tm),
        in_specs=in_specs,
        out_specs=out_specs,
        compiler_params=_cparams(("parallel", "arbitrary")),
        name="norm_in_proj",
    )(*args)


def _hgrn_chunk(qz, fz, vz, lb, st_ref, rev, chunk):
    w = A_WIDTH
    row = lax.broadcasted_iota(jnp.int32, (chunk, w), 0)
    tau = (chunk - 1 - row) if rev else row

    def prev(x, k):
        return pltpu.roll(x, (chunk - k) if rev else k, axis=0)

    def nxt(x, k):
        return pltpu.roll(x, k if rev else (chunk - k), axis=0)

    q = qz * jax.nn.sigmoid(qz)
    lbf = jnp.maximum(lb, LB_FLOOR)
    e = jnp.exp(-jnp.abs(fz))
    r = 1.0 / (1.0 + e)
    er = e * r
    pos = fz >= 0
    f = lbf + (1.0 - lbf) * jnp.where(pos, r, er)
    kk = (1.0 - lbf) * jnp.where(pos, er, r)
    b = jnp.log(f)
    step = 1
    while step < chunk:
        b = b + jnp.where(tau >= step, prev(b, step), 0.0)
        step *= 2

    ri = lax.broadcasted_iota(jnp.int32, (chunk, chunk), 0)
    ci = lax.broadcasted_iota(jnp.int32, (chunk, chunk), 1)
    if rev:
        ri = chunk - 1 - ri
        ci = chunk - 1 - ci

    qb16 = q.astype(BF16)
    kb16 = kk.astype(BF16)
    att = []
    for hh in range(A_HEADS):
        sl = slice(hh * HD, (hh + 1) * HD)
        att.append(jnp.where(ri == ci, _dot_nt(qb16[:, sl], kb16[:, sl]), 0.0))

    blk_end = b
    for level in range(int(math.log2(chunk))):
        c = 1 << level
        upper = (tau & c) != 0
        qt = (q * jnp.exp(jnp.where(upper, b - prev(blk_end, c), MASK_NEG))).astype(BF16)
        kt = (kk * jnp.exp(jnp.where(upper, MASK_NEG, blk_end - b))).astype(BF16)
        same_parent = (ri >> (level + 1)) == (ci >> (level + 1))
        for hh in range(A_HEADS):
            sl = slice(hh * HD, (hh + 1) * HD)
            att[hh] = att[hh] + jnp.where(same_parent, _dot_nt(qt[:, sl], kt[:, sl]), 0.0)
        blk_end = jnp.where(upper, blk_end, nxt(blk_end, c))

    qin = (q * jnp.exp(b)).astype(BF16)
    kend = (kk * jnp.exp(blk_end - b)).astype(BF16)
    vb = vz.astype(BF16)
    decay = jnp.exp(blk_end[0:1, :])
    outs = []
    for hh in range(A_HEADS):
        sl = slice(hh * HD, (hh + 1) * HD)
        st = st_ref[hh]
        o = _dot_nt(qin[:, sl], st.astype(BF16)) + _dot(att[hh].astype(BF16), vb[:, sl])
        st_ref[hh] = st * decay[:, sl] + _dot_tn(vb[:, sl], kend[:, sl])
        outs.append(o)
    return jnp.concatenate(outs, axis=1)


def _hgrn_kernel(qf_ref, ff_ref, vf_ref, qb_ref, fb_ref, vb_ref, lb_ref, of_ref, ob_ref, st_ref, *, chunk):
    @pl.when(pl.program_id(1) == 0)
    def _():
        st_ref[...] = jnp.zeros_like(st_ref)

    of_ref[0] = _hgrn_chunk(qf_ref[0], ff_ref[0], vf_ref[0], lb_ref[0:1, :], st_ref.at[0], False, chunk)
    ob_ref[0] = _hgrn_chunk(qb_ref[0], fb_ref[0], vb_ref[0], lb_ref[1:2, :], st_ref.at[1], True, chunk)


def _hgrn_call(hg, lower, ctx_len):
    bsz, s, _ = hg.shape
    chunk = HGRN_CHUNK
    n = s // chunk
    nctx = ctx_len // chunk
    w = A_WIDTH

    def rmap(i):
        return jnp.where(i < nctx, nctx - 1 - i, n + nctx - 1 - i)

    def fwd(col):
        return pl.BlockSpec((1, chunk, w), lambda b, i: (b, i, col))

    def bwd(col):
        return pl.BlockSpec((1, chunk, w), lambda b, i: (b, rmap(i), col))

    return pl.pallas_call(
        functools.partial(_hgrn_kernel, chunk=chunk),
        out_shape=[jax.ShapeDtypeStruct((bsz, s, w), F32)] * 2,
        grid=(bsz, n),
        in_specs=[fwd(0), fwd(1), fwd(3), bwd(0), bwd(2), bwd(3),
                  pl.BlockSpec((2, w), lambda b, i: (0, 0))],
        out_specs=[pl.BlockSpec((1, chunk, w), lambda b, i: (b, i, 0)),
                   pl.BlockSpec((1, chunk, w), lambda b, i: (b, rmap(i), 0))],
        scratch_shapes=[pltpu.VMEM((2, A_HEADS, HD, HD), F32)],
        compiler_params=_cparams(("parallel", "arbitrary")),
        name="hgrn2_scan",
    )(hg, hg, hg, hg, hg, hg, lower)


def _attn_kernel(q_ref, k_ref, v_ref, o_ref, m_ref, l_ref, acc_ref, *, tq, tk, nctx_q, ctx_len):
    qi = pl.program_id(2)
    ki = pl.program_id(3)
    is_ctx = qi < nctx_q

    @pl.when(ki == 0)
    def _():
        m_ref[...] = jnp.full_like(m_ref, -jnp.inf)
        l_ref[...] = jnp.zeros_like(l_ref)
        acc_ref[...] = jnp.zeros_like(acc_ref)

    @pl.when(jnp.logical_or(ki == 0, jnp.logical_not(is_ctx)))
    def _():
        q = q_ref[0].reshape(C_GROUP * tq, HD)
        s = _dot_nt(k_ref[0, 0], q)
        kpos = ki * tk + lax.broadcasted_iota(jnp.int32, (tk, 1), 0)
        s = jnp.where(jnp.logical_and(is_ctx, kpos >= ctx_len), MASK_NEG, s)
        m_prev = m_ref[...]
        m_new = jnp.maximum(m_prev, jnp.max(s, axis=0, keepdims=True))
        alpha = jnp.exp(m_prev - m_new)
        p = jnp.exp(s - m_new)
        l_ref[...] = alpha * l_ref[...] + jnp.sum(p, axis=0, keepdims=True)
        acc_ref[...] = alpha * acc_ref[...] + _dot_tn(v_ref[0, 0], p.astype(BF16))
        m_ref[...] = m_new

    @pl.when(ki == pl.num_programs(3) - 1)
    def _():
        ot = acc_ref[...] / l_ref[...]
        stacked = jnp.concatenate([ot[:, g * tq:(g + 1) * tq] for g in range(C_GROUP)], axis=0)
        o_ref[0] = stacked.T.astype(o_ref.dtype)


def _attn_call(q, k, v, ctx_len, tk):
    bsz, _, s, _ = q.shape
    tq = ATTN_TQ
    nq = s // tq
    nk = s // tk
    nctx_q = ctx_len // tq

    def kvmap(b, g, qi, ki):
        return (b, g, jnp.where(qi < nctx_q, 0, ki), 0)

    return pl.pallas_call(
        functools.partial(_attn_kernel, tq=tq, tk=tk, nctx_q=nctx_q, ctx_len=ctx_len),
        out_shape=jax.ShapeDtypeStruct((bsz, s, C_HEADS * HD), BF16),
        grid=(bsz, C_KV_HEADS, nq, nk),
        in_specs=[
            pl.BlockSpec((1, C_GROUP, tq, HD), lambda b, g, qi, ki: (b, g, qi, 0)),
            pl.BlockSpec((1, 1, tk, HD), kvmap),
            pl.BlockSpec((1, 1, tk, HD), kvmap),
        ],
        out_specs=pl.BlockSpec((1, tq, C_GROUP * HD), lambda b, g, qi, ki: (b, qi, g)),
        scratch_shapes=[pltpu.VMEM((1, C_GROUP * tq), F32), pltpu.VMEM((1, C_GROUP * tq), F32),
                        pltpu.VMEM((HD, C_GROUP * tq), F32)],
        compiler_params=_cparams(("parallel", "parallel", "arbitrary", "arbitrary")),
        name="gqa_attention",
    )(q, k, v)


def _route(logits_t, bias):
    scores = jax.nn.sigmoid(logits_t)
    sel = scores + bias
    rows = [sel[e:e + 1, :] for e in range(N_EXPERTS)]
    srow = [scores[e:e + 1, :] for e in range(N_EXPERTS)]
    best_val = None
    best_grp = None
    for g in range(N_GROUPS):
        a = rows[g * E_PER_GROUP:(g + 1) * E_PER_GROUP]
        m1 = a[0]
        i1 = jnp.zeros_like(a[0], dtype=jnp.int32)
        for j in range(1, E_PER_GROUP):
            take = a[j] > m1
            m1 = jnp.where(take, a[j], m1)
            i1 = jnp.where(take, j, i1)
        m2 = jnp.full_like(m1, -jnp.inf)
        for j in range(E_PER_GROUP):
            m2 = jnp.where(i1 == j, m2, jnp.maximum(m2, a[j]))
        gs = m1 + m2
        if g == 0:
            best_val, best_grp = gs, jnp.zeros_like(i1)
        else:
            take = gs > best_val
            best_val = jnp.where(take, gs, best_val)
            best_grp = jnp.where(take, g, best_grp)
    masked = [jnp.where(best_grp == (e // E_PER_GROUP), rows[e], NEG_BIG) for e in range(N_EXPERTS)]
    v1 = masked[0]
    e1 = jnp.zeros_like(best_grp)
    for e in range(1, N_EXPERTS):
        take = masked[e] > v1
        v1 = jnp.where(take, masked[e], v1)
        e1 = jnp.where(take, e, e1)
    v2 = jnp.full_like(v1, -jnp.inf)
    e2 = jnp.zeros_like(best_grp)
    for e in range(N_EXPERTS):
        take = jnp.logical_and(e1 != e, masked[e] > v2)
        v2 = jnp.where(take, masked[e], v2)
        e2 = jnp.where(take, e, e2)
    w1 = jnp.zeros_like(v1)
    w2 = jnp.zeros_like(v1)
    for e in range(N_EXPERTS):
        w1 = jnp.where(e1 == e, srow[e], w1)
        w2 = jnp.where(e2 == e, srow[e], w2)
    tot = w1 + w2
    return jnp.concatenate([e1, e2], axis=0), jnp.concatenate([w1 / tot, w2 / tot], axis=0)


def _out_kernel(x_ref, of_ref, ob_ref, g_ref, pc_ref, pp_ref, pn_ref, oc_ref, mod_ref, ng_ref, pw_ref,
                ps_ref, wo_ref, g2_ref, rw_ref, rb_ref, bda_ref,
                xo_ref, h2_ref, idx_ref, wgt_ref, *, tm, seg_tiles):
    j = pl.program_id(1)
    o = of_ref[0] + ob_ref[0]
    gz = g_ref[0]
    oa = o * lax.rsqrt(_seg_mean_sq(o, bda_ref[...]) + EPS) * ng_ref[...] * (gz * jax.nn.sigmoid(gz))
    nctx, ntot = seg_tiles
    first = jnp.logical_or(j == 0, j == nctx)
    last = jnp.logical_or(j == nctx - 1, j == ntot - 1)
    cur = pc_ref[0]
    ext = jnp.concatenate([jnp.where(first, 0.0, pp_ref[0]), cur, jnp.where(last, 0.0, pn_ref[0])], axis=0)
    n = tm + 2 * POOL_HALO
    s2 = ext + pltpu.roll(ext, 1, axis=0)
    s4 = pltpu.roll(s2, 1, axis=0) + pltpu.roll(s2, n - 1, axis=0)
    s8 = pltpu.roll(s4, 2, axis=0) + pltpu.roll(s4, n - 2, axis=0)
    s16 = pltpu.roll(s8, 4, axis=0) + pltpu.roll(s8, n - 4, axis=0)
    sums = [a[POOL_HALO:POOL_HALO + tm, :] for a in (s2, s4, s8, s16)]
    seg_start = jnp.where(j < nctx, 0, nctx) * tm
    seg_len = jnp.where(j < nctx, nctx, ntot - nctx) * tm
    t = j * tm - seg_start + lax.broadcasted_iota(jnp.int32, (tm, 1), 0)
    lane = lax.broadcasted_iota(jnp.int32, (tm, 256), 1)
    mean = jnp.zeros((tm, 256), F32)
    for gi, win in enumerate(POOL_WINDOWS):
        cnt = (jnp.minimum(t + win // 2, seg_len) - jnp.maximum(t - win // 2, 0)).astype(F32)
        mean = jnp.where((lane >> 6) == gi, sums[gi] / cnt, mean)
    ob = _dot((mean - cur).astype(BF16), pw_ref[...]) * ps_ref[...]
    wo = wo_ref
    mix = (_dot(oa.astype(BF16), wo[0:256, :]) + _dot(ob.astype(BF16), wo[256:512, :])
           + _dot(oc_ref[0], wo[512:1024, :]))
    x = x_ref[0] + mod_ref[0, 0, 2:3, :] * mix
    xo_ref[0] = x
    ms = jnp.mean(x * x, axis=-1, keepdims=True)
    h2 = x * lax.rsqrt(ms + EPS) * g2_ref[...]
    h2 = h2 * (1.0 + mod_ref[0, 0, 4:5, :]) + mod_ref[0, 0, 3:4, :]
    hhi, hlo = _split_bf16(h2)
    h2_ref[0] = hhi
    rhi, rlo = _split_bf16(rw_ref[...])
    logits_t = _dot_nt(rhi, hhi) + _dot_nt(rhi, hlo) + _dot_nt(rlo, hhi)
    ids, wts = _route(logits_t, rb_ref[...])
    idx_ref[0] = ids
    wgt_ref[0] = wts


def _out_call(xs, of, ob, hg, pool, oc, mods, ng, pw, ps, wo, g2, rw_t, rb, bda, ctx_len):
    bsz, s, d = xs.shape
    tm = TOKEN_TILE
    nt = s // tm
    nctx = ctx_len // tm
    hb = tm // POOL_HALO
    nhb = s // POOL_HALO

    def tok(b, j):
        return (b, j, 0)

    def modmap(b, j):
        return (0, jnp.where(j < nctx, bsz, b), 0, 0)

    const2 = lambda b, j: (0, 0)
    return pl.pallas_call(
        functools.partial(_out_kernel, tm=tm, seg_tiles=(nctx, nt)),
        out_shape=[
            jax.ShapeDtypeStruct((bsz, s, d), F32),
            jax.ShapeDtypeStruct((bsz, s, d), BF16),
            jax.ShapeDtypeStruct((bsz, 2, s), jnp.int32),
            jax.ShapeDtypeStruct((bsz, 2, s), F32),
        ],
        grid=(bsz, nt),
        in_specs=[
            pl.BlockSpec((1, tm, d), tok),
            pl.BlockSpec((1, tm, A_WIDTH), tok),
            pl.BlockSpec((1, tm, A_WIDTH), tok),
            pl.BlockSpec((1, tm, A_WIDTH), lambda b, j: (b, j, 4)),
            pl.BlockSpec((1, tm, 256), tok),
            pl.BlockSpec((1, POOL_HALO, 256), lambda b, j: (b, jnp.maximum(j * hb - 1, 0), 0)),
            pl.BlockSpec((1, POOL_HALO, 256), lambda b, j: (b, jnp.minimum((j + 1) * hb, nhb - 1), 0)),
            pl.BlockSpec((1, tm, C_HEADS * HD), tok),
            pl.BlockSpec((1, 1, 8, d), modmap),
            pl.BlockSpec((1, A_WIDTH), const2),
            pl.BlockSpec((256, 256), const2),
            pl.BlockSpec((1, 256), const2),
            pl.BlockSpec((d, d), const2),
            pl.BlockSpec((1, d), const2),
            pl.BlockSpec((N_EXPERTS, d), const2),
            pl.BlockSpec((N_EXPERTS, 1), const2),
            pl.BlockSpec((A_WIDTH, A_WIDTH), const2),
        ],
        out_specs=[
            pl.BlockSpec((1, tm, d), tok),
            pl.BlockSpec((1, tm, d), tok),
            pl.BlockSpec((1, 2, tm), lambda b, j: (b, 0, j)),
            pl.BlockSpec((1, 2, tm), lambda b, j: (b, 0, j)),
        ],
        compiler_params=_cparams(("parallel", "arbitrary")),
        name="mix_out_proj_router",
    )(xs, of, ob, hg, pool, pool, pool, oc, mods, ng, pw, ps, wo, g2, rw_t, rb, bda)


def _moe_kernel(be_ref, nb_ref, x_ref, wg_ref, wu_ref, wd_ref, y_ref, wgb, wub, wdb):
    i = pl.program_id(0)
    prev = be_ref[jnp.maximum(i - 1, 0)]
    fresh = jnp.logical_or(i == 0, be_ref[i] != prev)

    @pl.when(fresh)
    def _():
        wgb[...] = wg_ref[0].astype(BF16)
        wub[...] = wu_ref[0].astype(BF16)
        wdb[...] = wd_ref[0].astype(BF16)

    @pl.when(i < nb_ref[0])
    def _():
        x = x_ref[...]
        gate = _dot(x, wgb[...])
        up = _dot(x, wub[...])
        a = gate * jax.nn.sigmoid(gate) * up
        y_ref[...] = _dot(a.astype(BF16), wdb[...]).astype(y_ref.dtype)

    @pl.when(i >= nb_ref[0])
    def _():
        y_ref[...] = jnp.zeros_like(y_ref)


def _moe_call(block_expert, nblocks, xg, wg, wu, wd):
    p, d = xg.shape
    f = wg.shape[-1]
    bm = MOE_BM
    grid_spec = pltpu.PrefetchScalarGridSpec(
        num_scalar_prefetch=2,
        grid=(p // bm,),
        in_specs=[
            pl.BlockSpec((bm, d), lambda i, be, nb: (i, 0)),
            pl.BlockSpec((1, d, f), lambda i, be, nb: (be[i], 0, 0)),
            pl.BlockSpec((1, d, f), lambda i, be, nb: (be[i], 0, 0)),
            pl.BlockSpec((1, f, d), lambda i, be, nb: (be[i], 0, 0)),
        ],
        out_specs=pl.BlockSpec((bm, d), lambda i, be, nb: (i, 0)),
        scratch_shapes=[pltpu.VMEM((d, f), BF16), pltpu.VMEM((d, f), BF16), pltpu.VMEM((f, d), BF16)],
    )
    return pl.pallas_call(
        _moe_kernel,
        out_shape=jax.ShapeDtypeStruct((p, d), BF16),
        grid_spec=grid_spec,
        compiler_params=_cparams(("arbitrary",)),
        name="moe_experts",
    )(block_expert, nblocks, xg, wg, wu, wd)


def _moe_plan(idx, n_tokens):
    bm = MOE_BM
    flat = idx.reshape(-1)
    onehot = (flat[:, None] == jnp.arange(N_EXPERTS)[None, :]).astype(jnp.int32)
    rank = jnp.cumsum(onehot, axis=0) - onehot
    counts = jnp.sum(onehot, axis=0)
    padded = ((counts + bm - 1) // bm) * bm
    seg_end = jnp.cumsum(padded)
    seg_start = seg_end - padded
    slot = jnp.sum(onehot * (seg_start[None, :] + rank), axis=1)
    n_assign = flat.shape[0]
    p = (n_assign // bm + N_EXPERTS) * bm
    nb = p // bm
    blk_start = jnp.arange(nb) * bm
    block_expert = jnp.sum((blk_start[:, None] >= seg_end[None, :]).astype(jnp.int32), axis=1)
    nblocks = (seg_end[-1] // bm).astype(jnp.int32)
    last_e = jnp.max(jnp.where(counts > 0, jnp.arange(N_EXPERTS), 0))
    block_expert = jnp.minimum(block_expert, last_e).astype(jnp.int32)
    return slot, block_expert, nblocks.reshape(1), p


def _final_kernel(x_ref, m_ref, mod_ref, o_ref):
    o_ref[0] = x_ref[0] + mod_ref[0, 0, 5:6, :] * m_ref[0].astype(F32)


def _final_call(xs, m, mods, ctx_len):
    bsz, s, d = xs.shape
    tm = TOKEN_TILE
    nctx = ctx_len // tm
    t = s - ctx_len
    return pl.pallas_call(
        _final_kernel,
        out_shape=jax.ShapeDtypeStruct((bsz, t, d), F32),
        grid=(bsz, t // tm),
        in_specs=[
            pl.BlockSpec((1, tm, d), lambda b, j: (b, j + nctx, 0)),
            pl.BlockSpec((1, tm, d), lambda b, j: (b, j + nctx, 0)),
            pl.BlockSpec((1, 1, 8, d), lambda b, j: (0, b, 0, 0)),
        ],
        out_specs=pl.BlockSpec((1, tm, d), lambda b, j: (b, j, 0)),
        compiler_params=_cparams(("parallel", "arbitrary")),
        name="final_residual",
    )(xs, m, mods)


def _block_diag_ones(width):
    i = np.arange(width) // HD
    return jnp.asarray((i[:, None] == i[None, :]).astype(np.float32), dtype=BF16)


def _rope_tables(t, ctx_len):
    rows_n = t // GRID_W
    row = jnp.repeat(jnp.arange(rows_n), GRID_W).astype(F32)
    col = jnp.tile(jnp.arange(GRID_W), rows_n).astype(F32)
    n_freq = HD // 4
    inv = ROPE_THETA ** (-jnp.arange(n_freq, dtype=F32) / n_freq)
    ang = jnp.concatenate([row[:, None] * inv, col[:, None] * inv], axis=-1)
    cos, sin = jnp.cos(ang), jnp.sin(ang)
    cos_h = jnp.concatenate([cos, cos], axis=-1)
    sin_h = jnp.concatenate([-sin, sin], axis=-1)
    cos_h = jnp.concatenate([jnp.ones((ctx_len, HD), F32), cos_h], axis=0)
    sin_h = jnp.concatenate([jnp.zeros((ctx_len, HD), F32), sin_h], axis=0)
    return jnp.tile(cos_h, (1, 2)), jnp.tile(sin_h, (1, 2))


def _attn_key_tile(s):
    for tk in (768, 512, 1024, 384, 256, 128):
        if s % tk == 0:
            return tk
    raise ValueError("sequence length must be a multiple of 128")


def kernel(x, c, ctx, c_ctx, w_mod, b_mod, norm1_g, norm2_g, w_in, w_out, hgrn_lb, hgrn_norm_g,
           pool_w, pool_scale, q_norm_g, k_norm_g, router_w, router_b, moe_w_gate, moe_w_up, moe_w_down):
    bsz, t, d = x.shape
    ctx_len = ctx.shape[1]
    depth = w_mod.shape[0]
    s = ctx_len + t
    assert ctx_len % TOKEN_TILE == 0 and t % TOKEN_TILE == 0 and t % GRID_W == 0
    assert ctx_len % HGRN_CHUNK == 0 and ctx_len % ATTN_TQ == 0

    perm = np.concatenate([np.arange(0, HD, 2), np.arange(1, HD, 2)])
    na = 5 * A_WIDTH
    q0 = na + 256
    cols = np.arange(w_in.shape[-1])
    qk_cols = cols[q0:q0 + (C_HEADS + C_KV_HEADS) * HD].reshape(-1, HD)[:, perm].reshape(-1)
    cols = np.concatenate([cols[:q0], qk_cols, cols[q0 + qk_cols.size:]])
    w_in_b = w_in[:, :, cols].astype(BF16)
    w_out_b = w_out.astype(BF16)
    gq = jnp.tile(q_norm_g[:, perm], (1, C_HEADS)).reshape(depth, 1, C_HEADS * HD)
    gk = jnp.tile(k_norm_g[:, perm], (1, C_KV_HEADS)).reshape(depth, 1, C_KV_HEADS * HD)
    ng = jnp.tile(hgrn_norm_g, (1, A_HEADS)).reshape(depth, 1, A_WIDTH)
    gw = 256 // B_GROUPS
    eye = jnp.eye(B_GROUPS, dtype=F32)
    pw_bd = (eye[None, :, None, :, None] * pool_w[:, :, :, None, :]).reshape(depth, 256, 256).astype(BF16)
    sm = jax.nn.softmax(hgrn_lb.astype(F32), axis=0)
    lower = jnp.cumsum(sm, axis=0) - sm[0:1]
    cos_t, sin_t = _rope_tables(t, ctx_len)
    bdq, bdk, bda = _block_diag_ones(C_HEADS * HD), _block_diag_ones(C_KV_HEADS * HD), _block_diag_ones(A_WIDTH)
    rw_t = router_w.T
    rb = router_b.reshape(N_EXPERTS, 1).astype(F32)

    cond = jnp.zeros((8, d), F32).at[:bsz].set(c).at[bsz].set(c_ctx)
    mods = _mod_call(cond, w_mod, b_mod)
    mods = mods[:, :bsz + 1].reshape(depth, bsz + 1, 6, d)
    mods = jnp.pad(mods, ((0, 0), (0, 0), (0, 2), (0, 0)))

    xs = jnp.concatenate([ctx, x], axis=1)
    tk = _attn_key_tile(s)
    m = None
    for l in range(depth):
        outs = _in_call(xs, m, mods[l - 1:l] if l else None, mods[l:l + 1], norm1_g[l:l + 1], w_in_b[l],
                        cos_t, sin_t, gq[l], gk[l], bdq, bdk, ctx_len)
        if m is not None:
            xs, outs = outs[0], outs[1:]
        hg, pool, qh, kh, vh = outs
        of, ob = _hgrn_call(hg, lower[l], ctx_len)
        oc = _attn_call(qh, kh, vh, ctx_len, tk)
        xs, h2, idx, wgt = _out_call(xs, of, ob, hg, pool, oc, mods[l:l + 1], ng[l], pw_bd[l],
                                     pool_scale[l:l + 1], w_out_b[l], norm2_g[l:l + 1], rw_t, rb, bda, ctx_len)
        slot, block_expert, nblocks, p = _moe_plan(idx, bsz * s)
        tok_of_assign = (jnp.arange(bsz)[:, None, None] * s + jnp.arange(s)[None, None, :]
                         + jnp.zeros((1, 2, 1), jnp.int32)).reshape(-1)
        src = jnp.zeros((p,), jnp.int32).at[slot].set(tok_of_assign)
        xg = jnp.take(h2.reshape(bsz * s, d), src, axis=0)
        ys = _moe_call(block_expert, nblocks, xg, moe_w_gate[l], moe_w_up[l], moe_w_down[l])
        slot_b = slot.reshape(bsz, 2, s)
        y0 = jnp.take(ys, slot_b[:, 0].reshape(-1), axis=0).astype(F32)
        y1 = jnp.take(ys, slot_b[:, 1].reshape(-1), axis=0).astype(F32)
        m = (wgt[:, 0].reshape(-1, 1) * y0 + wgt[:, 1].reshape(-1, 1) * y1).reshape(bsz, s, d).astype(BF16)
    return _final_call(xs, m, mods[depth - 1:depth], ctx_len)
```

```python
import functools
import math

import jax
import jax.numpy as jnp
import numpy as np
from jax import lax
from jax.experimental import pallas as pl
from jax.experimental.pallas import tpu as pltpu

F32 = jnp.float32
BF16 = jnp.bfloat16

EPS = 1e-6
LB_FLOOR = 1e-30
NEG_BIG = -1e4
GRID_W = 64
ROPE_THETA = 10000.0
HD = 64
A_HEADS = 4
A_WIDTH = A_HEADS * HD
B_GROUPS = 4
POOL_WINDOWS = (2, 4, 8, 16)
POOL_HALO = 8
C_HEADS = 8
C_KV_HEADS = 2
C_GROUP = C_HEADS // C_KV_HEADS
N_EXPERTS = 16
N_GROUPS = 4
E_PER_GROUP = N_EXPERTS // N_GROUPS
MASK_NEG = -1e30

TOKEN_TILE = 256
HGRN_CHUNK = 128
ATTN_TQ = 256
ATTN_UNROLL = 11
LOG2E = 1.4426950408889634
BOUND_SLACK = 1.02
BOUND_MAX = 48.0
MOE_BM = 256
VMEM_LIMIT = 56 * 1024 * 1024


def _cparams(sem):
    return pltpu.CompilerParams(dimension_semantics=sem, vmem_limit_bytes=VMEM_LIMIT)


def _dot(a, b):
    return jnp.dot(a, b, preferred_element_type=F32)


def _dot_nt(a, b):
    return lax.dot_general(a, b, (((1,), (1,)), ((), ())), preferred_element_type=F32)


def _dot_tn(a, b):
    return lax.dot_general(a, b, (((0,), (0,)), ((), ())), preferred_element_type=F32)


def _split_bf16(x):
    hi = x.astype(BF16)
    lo = (x - hi.astype(F32)).astype(BF16)
    return hi, lo


def _seg_mean_sq(x, bd):
    hi, lo = _split_bf16(x * x)
    return (_dot(hi, bd) + _dot(lo, bd)) * (1.0 / HD)


def _swap_halves(x):
    w = x.shape[-1]
    lane = lax.broadcasted_iota(jnp.int32, x.shape, x.ndim - 1)
    up = pltpu.roll(x, w - HD // 2, axis=x.ndim - 1)
    dn = pltpu.roll(x, HD // 2, axis=x.ndim - 1)
    return jnp.where((lane & (HD // 2)) == 0, up, dn)


def _mod_kernel(c_ref, w_ref, b_ref, o_ref):
    c = c_ref[...]
    a = c * jax.nn.sigmoid(c)
    hi, lo = _split_bf16(a)
    whi, wlo = _split_bf16(w_ref[0])
    o_ref[0] = _dot(hi, whi) + _dot(lo, whi) + _dot(hi, wlo) + b_ref[0]


def _mod_call(cond, w_mod, b_mod):
    depth, d, n6 = w_mod.shape
    tn = 1536
    return pl.pallas_call(
        _mod_kernel,
        out_shape=jax.ShapeDtypeStruct((depth, 8, n6), F32),
        grid=(depth, n6 // tn),
        in_specs=[
            pl.BlockSpec((8, d), lambda l, j: (0, 0)),
            pl.BlockSpec((1, d, tn), lambda l, j: (l, 0, j)),
            pl.BlockSpec((1, 1, tn), lambda l, j: (l, 0, j)),
        ],
        out_specs=pl.BlockSpec((1, 8, tn), lambda l, j: (l, 0, j)),
        compiler_params=_cparams(("arbitrary", "arbitrary")),
        name="adaln_mod",
    )(cond, w_mod, b_mod.reshape(depth, 1, n6))


def _in_kernel(*refs, has_moe):
    if has_moe:
        (x_ref, m_ref, modp_ref, mod_ref, g_ref, w_ref, cos_ref, sin_ref, gq_ref, gk_ref,
         bdq_ref, bdk_ref, selq_ref, xo_ref, hg_ref, pool_ref, q_ref, k_ref, v_ref, qn2_ref, kn2_ref) = refs
        x = x_ref[0] + modp_ref[0, 0, 5:6, :] * m_ref[0].astype(F32)
        xo_ref[0] = x
    else:
        (x_ref, mod_ref, g_ref, w_ref, cos_ref, sin_ref, gq_ref, gk_ref,
         bdq_ref, bdk_ref, selq_ref, hg_ref, pool_ref, q_ref, k_ref, v_ref, qn2_ref, kn2_ref) = refs
        x = x_ref[0]
    ms = jnp.mean(x * x, axis=-1, keepdims=True)
    h = x * lax.rsqrt(ms + EPS) * g_ref[...]
    h = h * (1.0 + mod_ref[0, 0, 1:2, :]) + mod_ref[0, 0, 0:1, :]
    y = _dot(h.astype(BF16), w_ref[...])
    na = 5 * A_WIDTH
    hg_ref[0] = y[:, :na]
    pool_ref[0] = y[:, na:na + 256]
    q0 = na + 256
    qw = C_HEADS * HD
    kw = C_KV_HEADS * HD
    cos = cos_ref[...]
    sin = sin_ref[...]
    q = y[:, q0:q0 + qw]
    qn = q * lax.rsqrt(_seg_mean_sq(q, bdq_ref[...]) + EPS) * gq_ref[...]
    cos_q = jnp.concatenate([cos] * (qw // 128), axis=1)
    sin_q = jnp.concatenate([sin] * (qw // 128), axis=1)
    qr = (qn * cos_q + _swap_halves(qn) * sin_q) * (HD ** -0.5 * LOG2E)
    for hh in range(C_HEADS):
        q_ref[0, hh] = qr[:, hh * HD:(hh + 1) * HD].astype(BF16)
    k = y[:, q0 + qw:q0 + qw + kw]
    kn = k * lax.rsqrt(_seg_mean_sq(k, bdk_ref[...]) + EPS) * gk_ref[...]
    kr = kn * cos + _swap_halves(kn) * sin
    v = y[:, q0 + qw + kw:q0 + qw + 2 * kw]
    vt = v.T.astype(BF16)
    for hh in range(C_KV_HEADS):
        k_ref[0, hh] = kr[:, hh * HD:(hh + 1) * HD].astype(BF16)
        v_ref[0, hh, 0] = vt[hh * HD:(hh + 1) * HD, :]
    q2hi, q2lo = _split_bf16(qr * qr)
    qn2 = _dot_nt(selq_ref[...], q2hi) + _dot_nt(selq_ref[...], q2lo)
    for hh in range(C_KV_HEADS):
        qn2_ref[0, hh] = qn2[hh * C_GROUP:(hh + 1) * C_GROUP, :]
    k2hi, k2lo = _split_bf16(kr * kr)
    kn2 = _dot(k2hi, bdk_ref[...]) + _dot(k2lo, bdk_ref[...])
    kn2_ref[0, 0] = jnp.max(kn2, axis=0, keepdims=True)


def _in_call(xs, m, mods_prev, mods, g1, w_in, cos_t, sin_t, gq, gk, bdq, bdk, ctx_len):
    bsz, s, d = xs.shape
    tm = TOKEN_TILE
    nctx = ctx_len // tm
    ncols = w_in.shape[1]
    has_moe = m is not None

    def tok(b, j):
        return (b, j, 0)

    def modmap(b, j):
        return (0, jnp.where(j < nctx, bsz, b), 0, 0)

    const2 = lambda b, j: (0, 0)
    in_specs = [pl.BlockSpec((1, tm, d), tok)]
    args = [xs]
    if has_moe:
        in_specs += [pl.BlockSpec((1, tm, d), tok), pl.BlockSpec((1, 1, 8, d), modmap)]
        args += [m, mods_prev]
    in_specs += [
        pl.BlockSpec((1, 1, 8, d), modmap),
        pl.BlockSpec((1, d), const2),
        pl.BlockSpec((d, ncols), const2),
        pl.BlockSpec((tm, 128), lambda b, j: (j, 0)),
        pl.BlockSpec((tm, 128), lambda b, j: (j, 0)),
        pl.BlockSpec((1, C_HEADS * HD), const2),
        pl.BlockSpec((1, C_KV_HEADS * HD), const2),
        pl.BlockSpec((C_HEADS * HD, C_HEADS * HD), const2),
        pl.BlockSpec((C_KV_HEADS * HD, C_KV_HEADS * HD), const2),
        pl.BlockSpec((C_HEADS, C_HEADS * HD), const2),
    ]
    selq = bdq[::HD]
    args += [mods, g1, w_in, cos_t, sin_t, gq, gk, bdq, bdk, selq]
    out_shape = []
    out_specs = []
    if has_moe:
        out_shape.append(jax.ShapeDtypeStruct((bsz, s, d), F32))
        out_specs.append(pl.BlockSpec((1, tm, d), tok))
    out_shape += [
        jax.ShapeDtypeStruct((bsz, s, 5 * A_WIDTH), F32),
        jax.ShapeDtypeStruct((bsz, s, 256), F32),
        jax.ShapeDtypeStruct((bsz, C_HEADS, s, HD), BF16),
        jax.ShapeDtypeStruct((bsz, C_KV_HEADS, s, HD), BF16),
        jax.ShapeDtypeStruct((bsz, C_KV_HEADS, s // tm, HD, tm), BF16),
        jax.ShapeDtypeStruct((bsz, C_KV_HEADS, C_GROUP, s), F32),
        jax.ShapeDtypeStruct((bsz, s // tm, 1, C_KV_HEADS * HD), F32),
    ]
    out_specs += [
        pl.BlockSpec((1, tm, 5 * A_WIDTH), tok),
        pl.BlockSpec((1, tm, 256), tok),
        pl.BlockSpec((1, C_HEADS, tm, HD), lambda b, j: (b, 0, j, 0)),
        pl.BlockSpec((1, C_KV_HEADS, tm, HD), lambda b, j: (b, 0, j, 0)),
        pl.BlockSpec((1, C_KV_HEADS, 1, HD, tm), lambda b, j: (b, 0, j, 0, 0)),
        pl.BlockSpec((1, C_KV_HEADS, C_GROUP, tm), lambda b, j: (b, 0, 0, j)),
        pl.BlockSpec((1, 1, 1, C_KV_HEADS * HD), lambda b, j: (b, j, 0, 0)),
    ]
    return pl.pallas_call(
        functools.partial(_in_kernel, has_moe=has_moe),
        out_shape=out_shape,
        grid=(bsz, s // tm),
        in_specs=in_specs,
        out_specs=out_specs,
        compiler_params=_cparams(("parallel", "arbitrary")),
        name="norm_in_proj",
    )(*args)


def _hgrn_chunk(qz, fz, vz, lb, st_ref, rev, chunk):
    w = A_WIDTH
    row = lax.broadcasted_iota(jnp.int32, (chunk, w), 0)
    tau = (chunk - 1 - row) if rev else row

    def prev(x, k):
        return pltpu.roll(x, (chunk - k) if rev else k, axis=0)

    def nxt(x, k):
        return pltpu.roll(x, k if rev else (chunk - k), axis=0)

    q = qz * jax.nn.sigmoid(qz)
    lbf = jnp.maximum(lb, LB_FLOOR)
    e = jnp.exp(-jnp.abs(fz))
    r = 1.0 / (1.0 + e)
    er = e * r
    pos = fz >= 0
    f = lbf + (1.0 - lbf) * jnp.where(pos, r, er)
    kk = (1.0 - lbf) * jnp.where(pos, er, r)
    b = jnp.log(f)
    step = 1
    while step < chunk:
        b = b + jnp.where(tau >= step, prev(b, step), 0.0)
        step *= 2

    ri = lax.broadcasted_iota(jnp.int32, (chunk, chunk), 0)
    ci = lax.broadcasted_iota(jnp.int32, (chunk, chunk), 1)
    if rev:
        ri = chunk - 1 - ri
        ci = chunk - 1 - ci

    qb16 = q.astype(BF16)
    kb16 = kk.astype(BF16)
    att = []
    for hh in range(A_HEADS):
        sl = slice(hh * HD, (hh + 1) * HD)
        att.append(jnp.where(ri == ci, _dot_nt(qb16[:, sl], kb16[:, sl]), 0.0))

    blk_end = b
    for level in range(int(math.log2(chunk))):
        c = 1 << level
        upper = (tau & c) != 0
        qt = (q * jnp.exp(jnp.where(upper, b - prev(blk_end, c), MASK_NEG))).astype(BF16)
        kt = (kk * jnp.exp(jnp.where(upper, MASK_NEG, blk_end - b))).astype(BF16)
        same_parent = (ri >> (level + 1)) == (ci >> (level + 1))
        for hh in range(A_HEADS):
            sl = slice(hh * HD, (hh + 1) * HD)
            att[hh] = att[hh] + jnp.where(same_parent, _dot_nt(qt[:, sl], kt[:, sl]), 0.0)
        blk_end = jnp.where(upper, blk_end, nxt(blk_end, c))

    qin = (q * jnp.exp(b)).astype(BF16)
    kend = (kk * jnp.exp(blk_end - b)).astype(BF16)
    vb = vz.astype(BF16)
    decay = jnp.exp(blk_end[0:1, :])
    outs = []
    for hh in range(A_HEADS):
        sl = slice(hh * HD, (hh + 1) * HD)
        st = st_ref[hh]
        o = _dot_nt(qin[:, sl], st.astype(BF16)) + _dot(att[hh].astype(BF16), vb[:, sl])
        st_ref[hh] = st * decay[:, sl] + _dot_tn(vb[:, sl], kend[:, sl])
        outs.append(o)
    return jnp.concatenate(outs, axis=1)


def _hgrn_kernel(qf_ref, ff_ref, vf_ref, qb_ref, fb_ref, vb_ref, lb_ref, of_ref, ob_ref, st_ref, *, chunk):
    @pl.when(pl.program_id(1) == 0)
    def _():
        st_ref[...] = jnp.zeros_like(st_ref)

    of_ref[0] = _hgrn_chunk(qf_ref[0], ff_ref[0], vf_ref[0], lb_ref[0:1, :], st_ref.at[0], False, chunk)
    ob_ref[0] = _hgrn_chunk(qb_ref[0], fb_ref[0], vb_ref[0], lb_ref[1:2, :], st_ref.at[1], True, chunk)


def _hgrn_call(hg, lower, ctx_len):
    bsz, s, _ = hg.shape
    chunk = HGRN_CHUNK
    n = s // chunk
    nctx = ctx_len // chunk
    w = A_WIDTH

    def rmap(i):
        return jnp.where(i < nctx, nctx - 1 - i, n + nctx - 1 - i)

    def fwd(col):
        return pl.BlockSpec((1, chunk, w), lambda b, i: (b, i, col))

    def bwd(col):
        return pl.BlockSpec((1, chunk, w), lambda b, i: (b, rmap(i), col))

    return pl.pallas_call(
        functools.partial(_hgrn_kernel, chunk=chunk),
        out_shape=[jax.ShapeDtypeStruct((bsz, s, w), F32)] * 2,
        grid=(bsz, n),
        in_specs=[fwd(0), fwd(1), fwd(3), bwd(0), bwd(2), bwd(3),
                  pl.BlockSpec((2, w), lambda b, i: (0, 0))],
        out_specs=[pl.BlockSpec((1, chunk, w), lambda b, i: (b, i, 0)),
                   pl.BlockSpec((1, chunk, w), lambda b, i: (b, rmap(i), 0))],
        scratch_shapes=[pltpu.VMEM((2, A_HEADS, HD, HD), F32)],
        compiler_params=_cparams(("parallel", "arbitrary")),
        name="hgrn2_scan",
    )(hg, hg, hg, hg, hg, hg, lower)


def _attn_kernel(kmax_ref, q_ref, k_ref, vt_ref, qn2_ref, o_ref, *, tq, ts, n_sub, n_ctx_sub, nctx_q, unroll,
                 bounded):
    qi = pl.program_id(2)
    w = C_GROUP * tq
    q = q_ref[0].reshape(w, HD)

    def keys(c0, nc):
        return k_ref[0, 0, pl.ds(pl.multiple_of(c0 * ts, ts), nc * ts), :]

    def values_dot(c0, nc, p):
        pb = p.astype(BF16)
        out = _dot(vt_ref[0, 0, c0], pb[0:ts])
        for c in range(1, nc):
            out = out + _dot(vt_ref[0, 0, c0 + c], pb[c * ts:(c + 1) * ts])
        return out

    if bounded:
        kmax = kmax_ref[pl.program_id(0) * C_KV_HEADS + pl.program_id(1)]
        qn2 = jnp.concatenate([qn2_ref[0, 0, g:g + 1, :] for g in range(C_GROUP)], axis=1)
        shift = jnp.sqrt(qn2) * kmax

        def sub_tile(c0, nc, carry):
            l8, acc = carry
            p = jnp.exp2(_dot_nt(keys(c0, nc), q) - shift)
            l8 = l8 + jnp.sum(p.reshape(nc * ts // 8, 8, w), axis=0)
            return l8, acc + values_dot(c0, nc, p)

        init = (jnp.zeros((8, w), F32), jnp.zeros((HD, w), F32))
    else:
        def sub_tile(c0, nc, carry):
            m_prev, l_prev, acc = carry
            s = _dot_nt(keys(c0, nc), q)
            m_new = jnp.maximum(m_prev, jnp.max(s, axis=0, keepdims=True))
            alpha = jnp.exp2(m_prev - m_new)
            p = jnp.exp2(s - m_new)
            l_new = alpha * l_prev + jnp.sum(p, axis=0, keepdims=True)
            return m_new, l_new, alpha * acc + values_dot(c0, nc, p)

        init = (jnp.full((1, w), -jnp.inf, F32), jnp.zeros((1, w), F32), jnp.zeros((HD, w), F32))

    def run(n):
        carry = init
        if n >= unroll:
            carry = lax.fori_loop(0, n // unroll, lambda i, c: sub_tile(i * unroll, unroll, c), carry)
        if n % unroll:
            carry = sub_tile((n // unroll) * unroll, n % unroll, carry)
        l_fin = jnp.sum(carry[-2], axis=0, keepdims=True)
        ot = carry[-1] / l_fin
        stacked = jnp.concatenate([ot[:, g * tq:(g + 1) * tq] for g in range(C_GROUP)], axis=0)
        o_ref[0] = stacked.T.astype(o_ref.dtype)

    @pl.when(qi < nctx_q)
    def _():
        run(n_ctx_sub)

    @pl.when(qi >= nctx_q)
    def _():
        run(n_sub)


def _attn_call(kmax, q, k, vt, qn2, ctx_len, bounded):
    bsz, _, s, _ = q.shape
    tq = ATTN_TQ
    ts = vt.shape[-1]
    assert ctx_len % ts == 0
    n_sub = s // ts
    grid_spec = pltpu.PrefetchScalarGridSpec(
        num_scalar_prefetch=1,
        grid=(bsz, C_KV_HEADS, s // tq),
        in_specs=[
            pl.BlockSpec((1, C_GROUP, tq, HD), lambda b, g, qi, km: (b, g, qi, 0)),
            pl.BlockSpec((1, 1, s, HD), lambda b, g, qi, km: (b, g, 0, 0)),
            pl.BlockSpec((1, 1, n_sub, HD, ts), lambda b, g, qi, km: (b, g, 0, 0, 0)),
            pl.BlockSpec((1, 1, C_GROUP, tq), lambda b, g, qi, km: (b, g, 0, qi)),
        ],
        out_specs=pl.BlockSpec((1, tq, C_GROUP * HD), lambda b, g, qi, km: (b, qi, g)),
    )
    return pl.pallas_call(
        functools.partial(_attn_kernel, tq=tq, ts=ts, n_sub=n_sub, n_ctx_sub=ctx_len // ts,
                          nctx_q=ctx_len // tq, unroll=ATTN_UNROLL, bounded=bounded),
        out_shape=jax.ShapeDtypeStruct((bsz, s, C_HEADS * HD), BF16),
        grid_spec=grid_spec,
        compiler_params=_cparams(("parallel", "parallel", "arbitrary")),
        name="gqa_attention_bounded" if bounded else "gqa_attention",
    )(kmax, q, k, vt, qn2)


def _attention(q, k, vt, qn2, kn2, ctx_len):
    bsz = q.shape[0]
    k2 = jnp.max(kn2, axis=(1, 2)).reshape(bsz, C_KV_HEADS, HD)[:, :, 0]
    q2 = jnp.max(qn2.reshape(bsz, C_KV_HEADS, -1), axis=-1)
    kmax = (jnp.sqrt(k2) * BOUND_SLACK).reshape(-1)
    small = jnp.max(jnp.sqrt(q2).reshape(-1) * kmax) <= BOUND_MAX
    return lax.cond(small,
                    functools.partial(_attn_call, ctx_len=ctx_len, bounded=True),
                    functools.partial(_attn_call, ctx_len=ctx_len, bounded=False),
                    kmax, q, k, vt, qn2)


def _route(logits_t, bias):
    scores = jax.nn.sigmoid(logits_t)
    sel = scores + bias
    rows = [sel[e:e + 1, :] for e in range(N_EXPERTS)]
    srow = [scores[e:e + 1, :] for e in range(N_EXPERTS)]
    best_val = None
    best_grp = None
    for g in range(N_GROUPS):
        a = rows[g * E_PER_GROUP:(g + 1) * E_PER_GROUP]
        m1 = a[0]
        i1 = jnp.zeros_like(a[0], dtype=jnp.int32)
        for j in range(1, E_PER_GROUP):
            take = a[j] > m1
            m1 = jnp.where(take, a[j], m1)
            i1 = jnp.where(take, j, i1)
        m2 = jnp.full_like(m1, -jnp.inf)
        for j in range(E_PER_GROUP):
            m2 = jnp.where(i1 == j, m2, jnp.maximum(m2, a[j]))
        gs = m1 + m2
        if g == 0:
            best_val, best_grp = gs, jnp.zeros_like(i1)
        else:
            take = gs > best_val
            best_val = jnp.where(take, gs, best_val)
            best_grp = jnp.where(take, g, best_grp)
    masked = [jnp.where(best_grp == (e // E_PER_GROUP), rows[e], NEG_BIG) for e in range(N_EXPERTS)]
    v1 = masked[0]
    e1 = jnp.zeros_like(best_grp)
    for e in range(1, N_EXPERTS):
        take = masked[e] > v1
        v1 = jnp.where(take, masked[e], v1)
        e1 = jnp.where(take, e, e1)
    v2 = jnp.full_like(v1, -jnp.inf)
    e2 = jnp.zeros_like(best_grp)
    for e in range(N_EXPERTS):
        take = jnp.logical_and(e1 != e, masked[e] > v2)
        v2 = jnp.where(take, masked[e], v2)
        e2 = jnp.where(take, e, e2)
    w1 = jnp.zeros_like(v1)
    w2 = jnp.zeros_like(v1)
    for e in range(N_EXPERTS):
        w1 = jnp.where(e1 == e, srow[e], w1)
        w2 = jnp.where(e2 == e, srow[e], w2)
    tot = w1 + w2
    return jnp.concatenate([e1, e2], axis=0), jnp.concatenate([w1 / tot, w2 / tot], axis=0)


def _out_kernel(x_ref, of_ref, ob_ref, g_ref, pc_ref, pp_ref, pn_ref, oc_ref, mod_ref, ng_ref, pw_ref,
                ps_ref, wo_ref, g2_ref, rw_ref, rb_ref, bda_ref,
                xo_ref, h2_ref, idx_ref, wgt_ref, *, tm, seg_tiles):
    j = pl.program_id(1)
    o = of_ref[0] + ob_ref[0]
    gz = g_ref[0]
    oa = o * lax.rsqrt(_seg_mean_sq(o, bda_ref[...]) + EPS) * ng_ref[...] * (gz * jax.nn.sigmoid(gz))
    nctx, ntot = seg_tiles
    first = jnp.logical_or(j == 0, j == nctx)
    last = jnp.logical_or(j == nctx - 1, j == ntot - 1)
    cur = pc_ref[0]
    ext = jnp.concatenate([jnp.where(first, 0.0, pp_ref[0]), cur, jnp.where(last, 0.0, pn_ref[0])], axis=0)
    n = tm + 2 * POOL_HALO
    s2 = ext + pltpu.roll(ext, 1, axis=0)
    s4 = pltpu.roll(s2, 1, axis=0) + pltpu.roll(s2, n - 1, axis=0)
    s8 = pltpu.roll(s4, 2, axis=0) + pltpu.roll(s4, n - 2, axis=0)
    s16 = pltpu.roll(s8, 4, axis=0) + pltpu.roll(s8, n - 4, axis=0)
    sums = [a[POOL_HALO:POOL_HALO + tm, :] for a in (s2, s4, s8, s16)]
    seg_start = jnp.where(j < nctx, 0, nctx) * tm
    seg_len = jnp.where(j < nctx, nctx, ntot - nctx) * tm
    t = j * tm - seg_start + lax.broadcasted_iota(jnp.int32, (tm, 1), 0)
    lane = lax.broadcasted_iota(jnp.int32, (tm, 256), 1)
    mean = jnp.zeros((tm, 256), F32)
    for gi, win in enumerate(POOL_WINDOWS):
        cnt = (jnp.minimum(t + win // 2, seg_len) - jnp.maximum(t - win // 2, 0)).astype(F32)
        mean = jnp.where((lane >> 6) == gi, sums[gi] / cnt, mean)
    ob = _dot((mean - cur).astype(BF16), pw_ref[...]) * ps_ref[...]
    wo = wo_ref
    mix = (_dot(oa.astype(BF16), wo[0:256, :]) + _dot(ob.astype(BF16), wo[256:512, :])
           + _dot(oc_ref[0], wo[512:1024, :]))
    x = x_ref[0] + mod_ref[0, 0, 2:3, :] * mix
    xo_ref[0] = x
    ms = jnp.mean(x * x, axis=-1, keepdims=True)
    h2 = x * lax.rsqrt(ms + EPS) * g2_ref[...]
    h2 = h2 * (1.0 + mod_ref[0, 0, 4:5, :]) + mod_ref[0, 0, 3:4, :]
    hhi, hlo = _split_bf16(h2)
    h2_ref[0] = hhi
    rhi, rlo = _split_bf16(rw_ref[...])
    logits_t = _dot_nt(rhi, hhi) + _dot_nt(rhi, hlo) + _dot_nt(rlo, hhi)
    ids, wts = _route(logits_t, rb_ref[...])
    idx_ref[0] = ids
    wgt_ref[0] = wts


def _out_call(xs, of, ob, hg, pool, oc, mods, ng, pw, ps, wo, g2, rw_t, rb, bda, ctx_len):
    bsz, s, d = xs.shape
    tm = TOKEN_TILE
    nt = s // tm
    nctx = ctx_len // tm
    hb = tm // POOL_HALO
    nhb = s // POOL_HALO

    def tok(b, j):
        return (b, j, 0)

    def modmap(b, j):
        return (0, jnp.where(j < nctx, bsz, b), 0, 0)

    const2 = lambda b, j: (0, 0)
    return pl.pallas_call(
        functools.partial(_out_kernel, tm=tm, seg_tiles=(nctx, nt)),
        out_shape=[
            jax.ShapeDtypeStruct((bsz, s, d), F32),
            jax.ShapeDtypeStruct((bsz, s, d), BF16),
            jax.ShapeDtypeStruct((bsz, 2, s), jnp.int32),
            jax.ShapeDtypeStruct((bsz, 2, s), F32),
        ],
        grid=(bsz, nt),
        in_specs=[
            pl.BlockSpec((1, tm, d), tok),
            pl.BlockSpec((1, tm, A_WIDTH), tok),
            pl.BlockSpec((1, tm, A_WIDTH), tok),
            pl.BlockSpec((1, tm, A_WIDTH), lambda b, j: (b, j, 4)),
            pl.BlockSpec((1, tm, 256), tok),
            pl.BlockSpec((1, POOL_HALO, 256), lambda b, j: (b, jnp.maximum(j * hb - 1, 0), 0)),
            pl.BlockSpec((1, POOL_HALO, 256), lambda b, j: (b, jnp.minimum((j + 1) * hb, nhb - 1), 0)),
            pl.BlockSpec((1, tm, C_HEADS * HD), tok),
            pl.BlockSpec((1, 1, 8, d), modmap),
            pl.BlockSpec((1, A_WIDTH), const2),
            pl.BlockSpec((256, 256), const2),
            pl.BlockSpec((1, 256), const2),
            pl.BlockSpec((d, d), const2),
            pl.BlockSpec((1, d), const2),
            pl.BlockSpec((N_EXPERTS, d), const2),
            pl.BlockSpec((N_EXPERTS, 1), const2),
            pl.BlockSpec((A_WIDTH, A_WIDTH), const2),
        ],
        out_specs=[
            pl.BlockSpec((1, tm, d), tok),
            pl.BlockSpec((1, tm, d), tok),
            pl.BlockSpec((1, 2, tm), lambda b, j: (b, 0, j)),
            pl.BlockSpec((1, 2, tm), lambda b, j: (b, 0, j)),
        ],
        compiler_params=_cparams(("parallel", "arbitrary")),
        name="mix_out_proj_router",
    )(xs, of, ob, hg, pool, pool, pool, oc, mods, ng, pw, ps, wo, g2, rw_t, rb, bda)


def _moe_kernel(be_ref, nb_ref, x_ref, wg_ref, wu_ref, wd_ref, y_ref, wgb, wub, wdb):
    i = pl.program_id(0)
    prev = be_ref[jnp.maximum(i - 1, 0)]
    fresh = jnp.logical_or(i == 0, be_ref[i] != prev)

    @pl.when(fresh)
    def _():
        wgb[...] = wg_ref[0].astype(BF16)
        wub[...] = wu_ref[0].astype(BF16)
        wdb[...] = wd_ref[0].astype(BF16)

    @pl.when(i < nb_ref[0])
    def _():
        x = x_ref[...]
        gate = _dot(x, wgb[...])
        up = _dot(x, wub[...])
        a = gate * jax.nn.sigmoid(gate) * up
        y_ref[...] = _dot(a.astype(BF16), wdb[...]).astype(y_ref.dtype)

    @pl.when(i >= nb_ref[0])
    def _():
        y_ref[...] = jnp.zeros_like(y_ref)


def _moe_call(block_expert, nblocks, xg, wg, wu, wd):
    p, d = xg.shape
    f = wg.shape[-1]
    bm = MOE_BM
    grid_spec = pltpu.PrefetchScalarGridSpec(
        num_scalar_prefetch=2,
        grid=(p // bm,),
        in_specs=[
            pl.BlockSpec((bm, d), lambda i, be, nb: (i, 0)),
            pl.BlockSpec((1, d, f), lambda i, be, nb: (be[i], 0, 0)),
            pl.BlockSpec((1, d, f), lambda i, be, nb: (be[i], 0, 0)),
            pl.BlockSpec((1, f, d), lambda i, be, nb: (be[i], 0, 0)),
        ],
        out_specs=pl.BlockSpec((bm, d), lambda i, be, nb: (i, 0)),
        scratch_shapes=[pltpu.VMEM((d, f), BF16), pltpu.VMEM((d, f), BF16), pltpu.VMEM((f, d), BF16)],
    )
    return pl.pallas_call(
        _moe_kernel,
        out_shape=jax.ShapeDtypeStruct((p, d), BF16),
        grid_spec=grid_spec,
        compiler_params=_cparams(("arbitrary",)),
        name="moe_experts",
    )(block_expert, nblocks, xg, wg, wu, wd)


def _moe_plan(idx, n_tokens):
    bm = MOE_BM
    flat = idx.reshape(-1)
    onehot = (flat[:, None] == jnp.arange(N_EXPERTS)[None, :]).astype(jnp.int32)
    rank = jnp.cumsum(onehot, axis=0) - onehot
    counts = jnp.sum(onehot, axis=0)
    padded = ((counts + bm - 1) // bm) * bm
    seg_end = jnp.cumsum(padded)
    seg_start = seg_end - padded
    slot = jnp.sum(onehot * (seg_start[None, :] + rank), axis=1)
    n_assign = flat.shape[0]
    p = (n_assign // bm + N_EXPERTS) * bm
    nb = p // bm
    blk_start = jnp.arange(nb) * bm
    block_expert = jnp.sum((blk_start[:, None] >= seg_end[None, :]).astype(jnp.int32), axis=1)
    nblocks = (seg_end[-1] // bm).astype(jnp.int32)
    last_e = jnp.max(jnp.where(counts > 0, jnp.arange(N_EXPERTS), 0))
    block_expert = jnp.minimum(block_expert, last_e).astype(jnp.int32)
    return slot, block_expert, nblocks.reshape(1), p


def _final_kernel(x_ref, m_ref, mod_ref, o_ref):
    o_ref[0] = x_ref[0] + mod_ref[0, 0, 5:6, :] * m_ref[0].astype(F32)


def _final_call(xs, m, mods, ctx_len):
    bsz, s, d = xs.shape
    tm = TOKEN_TILE
    nctx = ctx_len // tm
    t = s - ctx_len
    return pl.pallas_call(
        _final_kernel,
        out_shape=jax.ShapeDtypeStruct((bsz, t, d), F32),
        grid=(bsz, t // tm),
        in_specs=[
            pl.BlockSpec((1, tm, d), lambda b, j: (b, j + nctx, 0)),
            pl.BlockSpec((1, tm, d), lambda b, j: (b, j + nctx, 0)),
            pl.BlockSpec((1, 1, 8, d), lambda b, j: (0, b, 0, 0)),
        ],
        out_specs=pl.BlockSpec((1, tm, d), lambda b, j: (b, j, 0)),
        compiler_params=_cparams(("parallel", "arbitrary")),
        name="final_residual",
    )(xs, m, mods)


def _block_diag_ones(width):
    i = np.arange(width) // HD
    return jnp.asarray((i[:, None] == i[None, :]).astype(np.float32), dtype=BF16)


def _rope_tables(t, ctx_len):
    rows_n = t // GRID_W
    row = jnp.repeat(jnp.arange(rows_n), GRID_W).astype(F32)
    col = jnp.tile(jnp.arange(GRID_W), rows_n).astype(F32)
    n_freq = HD // 4
    inv = ROPE_THETA ** (-jnp.arange(n_freq, dtype=F32) / n_freq)
    ang = jnp.concatenate([row[:, None] * inv, col[:, None] * inv], axis=-1)
    cos, sin = jnp.cos(ang), jnp.sin(ang)
    cos_h = jnp.concatenate([cos, cos], axis=-1)
    sin_h = jnp.concatenate([-sin, sin], axis=-1)
    cos_h = jnp.concatenate([jnp.ones((ctx_len, HD), F32), cos_h], axis=0)
    sin_h = jnp.concatenate([jnp.zeros((ctx_len, HD), F32), sin_h], axis=0)
    return jnp.tile(cos_h, (1, 2)), jnp.tile(sin_h, (1, 2))


def kernel(x, c, ctx, c_ctx, w_mod, b_mod, norm1_g, norm2_g, w_in, w_out, hgrn_lb, hgrn_norm_g,
           pool_w, pool_scale, q_norm_g, k_norm_g, router_w, router_b, moe_w_gate, moe_w_up, moe_w_down):
    bsz, t, d = x.shape
    ctx_len = ctx.shape[1]
    depth = w_mod.shape[0]
    s = ctx_len + t
    assert ctx_len % TOKEN_TILE == 0 and t % TOKEN_TILE == 0 and t % GRID_W == 0
    assert ctx_len % HGRN_CHUNK == 0 and ctx_len % ATTN_TQ == 0

    perm = np.concatenate([np.arange(0, HD, 2), np.arange(1, HD, 2)])
    na = 5 * A_WIDTH
    q0 = na + 256
    cols = np.arange(w_in.shape[-1])
    qk_cols = cols[q0:q0 + (C_HEADS + C_KV_HEADS) * HD].reshape(-1, HD)[:, perm].reshape(-1)
    cols = np.concatenate([cols[:q0], qk_cols, cols[q0 + qk_cols.size:]])
    w_in_b = w_in[:, :, cols].astype(BF16)
    w_out_b = w_out.astype(BF16)
    gq = jnp.tile(q_norm_g[:, perm], (1, C_HEADS)).reshape(depth, 1, C_HEADS * HD)
    gk = jnp.tile(k_norm_g[:, perm], (1, C_KV_HEADS)).reshape(depth, 1, C_KV_HEADS * HD)
    ng = jnp.tile(hgrn_norm_g, (1, A_HEADS)).reshape(depth, 1, A_WIDTH)
    gw = 256 // B_GROUPS
    eye = jnp.eye(B_GROUPS, dtype=F32)
    pw_bd = (eye[None, :, None, :, None] * pool_w[:, :, :, None, :]).reshape(depth, 256, 256).astype(BF16)
    sm = jax.nn.softmax(hgrn_lb.astype(F32), axis=0)
    lower = jnp.cumsum(sm, axis=0) - sm[0:1]
    cos_t, sin_t = _rope_tables(t, ctx_len)
    bdq, bdk, bda = _block_diag_ones(C_HEADS * HD), _block_diag_ones(C_KV_HEADS * HD), _block_diag_ones(A_WIDTH)
    rw_t = router_w.T
    rb = router_b.reshape(N_EXPERTS, 1).astype(F32)

    cond = jnp.zeros((8, d), F32).at[:bsz].set(c).at[bsz].set(c_ctx)
    mods = _mod_call(cond, w_mod, b_mod)
    mods = mods[:, :bsz + 1].reshape(depth, bsz + 1, 6, d)
    mods = jnp.pad(mods, ((0, 0), (0, 0), (0, 2), (0, 0)))

    xs = jnp.concatenate([ctx, x], axis=1)
    m = None
    for l in range(depth):
        outs = _in_call(xs, m, mods[l - 1:l] if l else None, mods[l:l + 1], norm1_g[l:l + 1], w_in_b[l],
                        cos_t, sin_t, gq[l], gk[l], bdq, bdk, ctx_len)
        if m is not None:
            xs, outs = outs[0], outs[1:]
        hg, pool, qh, kh, vh, qn2, kn2 = outs
        of, ob = _hgrn_call(hg, lower[l], ctx_len)
        oc = _attention(qh, kh, vh, qn2, kn2, ctx_len)
        xs, h2, idx, wgt = _out_call(xs, of, ob, hg, pool, oc, mods[l:l + 1], ng[l], pw_bd[l],
                                     pool_scale[l:l + 1], w_out_b[l], norm2_g[l:l + 1], rw_t, rb, bda, ctx_len)
        slot, block_expert, nblocks, p = _moe_plan(idx, bsz * s)
        tok_of_assign = (jnp.arange(bsz)[:, None, None] * s + jnp.arange(s)[None, None, :]
                         + jnp.zeros((1, 2, 1), jnp.int32)).reshape(-1)
        src = jnp.zeros((p,), jnp.int32).at[slot].set(tok_of_assign)
        xg = jnp.take(h2.reshape(bsz * s, d), src, axis=0)
        ys = _moe_call(block_expert, nblocks, xg, moe_w_gate[l], moe_w_up[l], moe_w_down[l])
        slot_b = slot.reshape(bsz, 2, s)
        y0 = jnp.take(ys, slot_b[:, 0].reshape(-1), axis=0).astype(F32)
        y1 = jnp.take(ys, slot_b[:, 1].reshape(-1), axis=0).astype(F32)
        m = (wgt[:, 0].reshape(-1, 1) * y0 + wgt[:, 1].reshape(-1, 1) * y1).reshape(bsz, s, d).astype(BF16)
    return _final_call(xs, m, mods[depth - 1:depth], ctx_len)
```

```python
import functools
import math

import jax
import jax.numpy as jnp
import numpy as np
from jax import lax
from jax.experimental import pallas as pl
from jax.experimental.pallas import tpu as pltpu
from jax.experimental.pallas import tpu_sc as plsc

F32 = jnp.float32
BF16 = jnp.bfloat16

EPS = 1e-6
LB_FLOOR = 1e-30
NEG_BIG = -1e4
GRID_W = 64
ROPE_THETA = 10000.0
HD = 64
A_HEADS = 4
A_WIDTH = A_HEADS * HD
B_GROUPS = 4
POOL_WINDOWS = (2, 4, 8, 16)
POOL_HALO = 8
C_HEADS = 8
C_KV_HEADS = 2
C_GROUP = C_HEADS // C_KV_HEADS
N_EXPERTS = 16
N_GROUPS = 4
E_PER_GROUP = N_EXPERTS // N_GROUPS
MASK_NEG = -1e30

TOKEN_TILE = 256
HGRN_CHUNK = 128
ATTN_TQ = 256
ATTN_UNROLL = 11
LOG2E = 1.4426950408889634
BOUND_SLACK = 1.02
BOUND_MAX = 48.0
MOE_BM = 256
MOE_NC = 4
MOE_CW = 256
SC_WINDOW = 128
VMEM_LIMIT = 56 * 1024 * 1024


def _cparams(sem):
    return pltpu.CompilerParams(dimension_semantics=sem, vmem_limit_bytes=VMEM_LIMIT)


def _dot(a, b):
    return jnp.dot(a, b, preferred_element_type=F32)


def _dot_nt(a, b):
    return lax.dot_general(a, b, (((1,), (1,)), ((), ())), preferred_element_type=F32)


def _dot_tn(a, b):
    return lax.dot_general(a, b, (((0,), (0,)), ((), ())), preferred_element_type=F32)


def _split_bf16(x):
    hi = x.astype(BF16)
    lo = (x - hi.astype(F32)).astype(BF16)
    return hi, lo


def _seg_mean_sq(x, bd):
    hi, lo = _split_bf16(x * x)
    return (_dot(hi, bd) + _dot(lo, bd)) * (1.0 / HD)


def _swap_halves(x):
    w = x.shape[-1]
    lane = lax.broadcasted_iota(jnp.int32, x.shape, x.ndim - 1)
    up = pltpu.roll(x, w - HD // 2, axis=x.ndim - 1)
    dn = pltpu.roll(x, HD // 2, axis=x.ndim - 1)
    return jnp.where((lane & (HD // 2)) == 0, up, dn)


def _mod_kernel(c_ref, w_ref, b_ref, o_ref):
    c = c_ref[...]
    a = c * jax.nn.sigmoid(c)
    hi, lo = _split_bf16(a)
    whi, wlo = _split_bf16(w_ref[0])
    o_ref[0] = _dot(hi, whi) + _dot(lo, whi) + _dot(hi, wlo) + b_ref[0]


def _mod_call(cond, w_mod, b_mod):
    depth, d, n6 = w_mod.shape
    tn = 1536
    return pl.pallas_call(
        _mod_kernel,
        out_shape=jax.ShapeDtypeStruct((depth, 8, n6), F32),
        grid=(depth, n6 // tn),
        in_specs=[
            pl.BlockSpec((8, d), lambda l, j: (0, 0)),
            pl.BlockSpec((1, d, tn), lambda l, j: (l, 0, j)),
            pl.BlockSpec((1, 1, tn), lambda l, j: (l, 0, j)),
        ],
        out_specs=pl.BlockSpec((1, 8, tn), lambda l, j: (l, 0, j)),
        compiler_params=_cparams(("arbitrary", "arbitrary")),
        name="adaln_mod",
    )(cond, w_mod, b_mod.reshape(depth, 1, n6))


def _in_kernel(*refs, has_moe):
    if has_moe:
        (x_ref, y_ref, wc_ref, modp_ref, mod_ref, g_ref, w_ref, cos_ref, sin_ref, gq_ref, gk_ref,
         bdq_ref, bdk_ref, selq_ref, xo_ref, hg_ref, pool_ref, q_ref, k_ref, v_ref, qn2_ref, kn2_ref) = refs
        x = x_ref[0] + modp_ref[0, 0, 5:6, :] * _moe_combine(y_ref, wc_ref)
        xo_ref[0] = x
    else:
        (x_ref, mod_ref, g_ref, w_ref, cos_ref, sin_ref, gq_ref, gk_ref,
         bdq_ref, bdk_ref, selq_ref, hg_ref, pool_ref, q_ref, k_ref, v_ref, qn2_ref, kn2_ref) = refs
        x = x_ref[0]
    ms = jnp.mean(x * x, axis=-1, keepdims=True)
    h = x * lax.rsqrt(ms + EPS) * g_ref[...]
    h = h * (1.0 + mod_ref[0, 0, 1:2, :]) + mod_ref[0, 0, 0:1, :]
    y = _dot(h.astype(BF16), w_ref[...])
    na = 5 * A_WIDTH
    hg_ref[0] = y[:, :na]
    pool_ref[0] = y[:, na:na + 256]
    q0 = na + 256
    qw = C_HEADS * HD
    kw = C_KV_HEADS * HD
    cos = cos_ref[...]
    sin = sin_ref[...]
    q = y[:, q0:q0 + qw]
    qn = q * lax.rsqrt(_seg_mean_sq(q, bdq_ref[...]) + EPS) * gq_ref[...]
    cos_q = jnp.concatenate([cos] * (qw // 128), axis=1)
    sin_q = jnp.concatenate([sin] * (qw // 128), axis=1)
    qr = (qn * cos_q + _swap_halves(qn) * sin_q) * (HD ** -0.5 * LOG2E)
    for hh in range(C_HEADS):
        q_ref[0, hh] = qr[:, hh * HD:(hh + 1) * HD].astype(BF16)
    k = y[:, q0 + qw:q0 + qw + kw]
    kn = k * lax.rsqrt(_seg_mean_sq(k, bdk_ref[...]) + EPS) * gk_ref[...]
    kr = kn * cos + _swap_halves(kn) * sin
    v = y[:, q0 + qw + kw:q0 + qw + 2 * kw]
    vt = v.T.astype(BF16)
    for hh in range(C_KV_HEADS):
        k_ref[0, hh] = kr[:, hh * HD:(hh + 1) * HD].astype(BF16)
        v_ref[0, hh, 0] = vt[hh * HD:(hh + 1) * HD, :]
    q2hi, q2lo = _split_bf16(qr * qr)
    qn2 = _dot_nt(selq_ref[...], q2hi) + _dot_nt(selq_ref[...], q2lo)
    for hh in range(C_KV_HEADS):
        qn2_ref[0, hh] = qn2[hh * C_GROUP:(hh + 1) * C_GROUP, :]
    k2hi, k2lo = _split_bf16(kr * kr)
    kn2 = _dot(k2hi, bdk_ref[...]) + _dot(k2lo, bdk_ref[...])
    kn2_ref[0, 0] = jnp.max(kn2, axis=0, keepdims=True)


def _in_call(xs, moe, mods_prev, mods, g1, w_in, cos_t, sin_t, gq, gk, bdq, bdk, ctx_len):
    bsz, s, d = xs.shape
    tm = TOKEN_TILE
    nctx = ctx_len // tm
    ncols = w_in.shape[1]
    has_moe = moe is not None

    def tok(b, j):
        return (b, j, 0)

    def modmap(b, j):
        return (0, jnp.where(j < nctx, bsz, b), 0, 0)

    const2 = lambda b, j: (0, 0)
    in_specs = [pl.BlockSpec((1, tm, d), tok)]
    args = [xs]
    if has_moe:
        in_specs += [pl.BlockSpec((MOE_NC, 1, 2, tm, MOE_CW), lambda b, j: (0, b, 0, j, 0)),
                     pl.BlockSpec((1, tm, 128), tok), pl.BlockSpec((1, 1, 8, d), modmap)]
        args += [moe[0], moe[1], mods_prev]
    in_specs += [
        pl.BlockSpec((1, 1, 8, d), modmap),
        pl.BlockSpec((1, d), const2),
        pl.BlockSpec((d, ncols), const2),
        pl.BlockSpec((tm, 128), lambda b, j: (j, 0)),
        pl.BlockSpec((tm, 128), lambda b, j: (j, 0)),
        pl.BlockSpec((1, C_HEADS * HD), const2),
        pl.BlockSpec((1, C_KV_HEADS * HD), const2),
        pl.BlockSpec((C_HEADS * HD, C_HEADS * HD), const2),
        pl.BlockSpec((C_KV_HEADS * HD, C_KV_HEADS * HD), const2),
        pl.BlockSpec((C_HEADS, C_HEADS * HD), const2),
    ]
    selq = bdq[::HD]
    args += [mods, g1, w_in, cos_t, sin_t, gq, gk, bdq, bdk, selq]
    out_shape = []
    out_specs = []
    if has_moe:
        out_shape.append(jax.ShapeDtypeStruct((bsz, s, d), F32))
        out_specs.append(pl.BlockSpec((1, tm, d), tok))
    out_shape += [
        jax.ShapeDtypeStruct((bsz, s, 5 * A_WIDTH), F32),
        jax.ShapeDtypeStruct((bsz, s, 256), F32),
        jax.ShapeDtypeStruct((bsz, C_HEADS, s, HD), BF16),
        jax.ShapeDtypeStruct((bsz, C_KV_HEADS, s, HD), BF16),
        jax.ShapeDtypeStruct((bsz, C_KV_HEADS, s // tm, HD, tm), BF16),
        jax.ShapeDtypeStruct((bsz, C_KV_HEADS, C_GROUP, s), F32),
        jax.ShapeDtypeStruct((bsz, s // tm, 1, C_KV_HEADS * HD), F32),
    ]
    out_specs += [
        pl.BlockSpec((1, tm, 5 * A_WIDTH), tok),
        pl.BlockSpec((1, tm, 256), tok),
        pl.BlockSpec((1, C_HEADS, tm, HD), lambda b, j: (b, 0, j, 0)),
        pl.BlockSpec((1, C_KV_HEADS, tm, HD), lambda b, j: (b, 0, j, 0)),
        pl.BlockSpec((1, C_KV_HEADS, 1, HD, tm), lambda b, j: (b, 0, j, 0, 0)),
        pl.BlockSpec((1, C_KV_HEADS, C_GROUP, tm), lambda b, j: (b, 0, 0, j)),
        pl.BlockSpec((1, 1, 1, C_KV_HEADS * HD), lambda b, j: (b, j, 0, 0)),
    ]
    return pl.pallas_call(
        functools.partial(_in_kernel, has_moe=has_moe),
        out_shape=out_shape,
        grid=(bsz, s // tm),
        in_specs=in_specs,
        out_specs=out_specs,
        compiler_params=_cparams(("parallel", "arbitrary")),
        name="norm_in_proj",
    )(*args)


def _hgrn_chunk(qz, fz, vz, lb, st_ref, rev, chunk):
    w = A_WIDTH
    row = lax.broadcasted_iota(jnp.int32, (chunk, w), 0)
    tau = (chunk - 1 - row) if rev else row

    def prev(x, k):
        return pltpu.roll(x, (chunk - k) if rev else k, axis=0)

    def nxt(x, k):
        return pltpu.roll(x, k if rev else (chunk - k), axis=0)

    q = qz * jax.nn.sigmoid(qz)
    lbf = jnp.maximum(lb, LB_FLOOR)
    e = jnp.exp(-jnp.abs(fz))
    r = 1.0 / (1.0 + e)
    er = e * r
    pos = fz >= 0
    f = lbf + (1.0 - lbf) * jnp.where(pos, r, er)
    kk = (1.0 - lbf) * jnp.where(pos, er, r)
    b = jnp.log(f)
    step = 1
    while step < chunk:
        b = b + jnp.where(tau >= step, prev(b, step), 0.0)
        step *= 2

    ri = lax.broadcasted_iota(jnp.int32, (chunk, chunk), 0)
    ci = lax.broadcasted_iota(jnp.int32, (chunk, chunk), 1)
    if rev:
        ri = chunk - 1 - ri
        ci = chunk - 1 - ci

    qb16 = q.astype(BF16)
    kb16 = kk.astype(BF16)
    att = []
    for hh in range(A_HEADS):
        sl = slice(hh * HD, (hh + 1) * HD)
        att.append(jnp.where(ri == ci, _dot_nt(qb16[:, sl], kb16[:, sl]), 0.0))

    blk_end = b
    for level in range(int(math.log2(chunk))):
        c = 1 << level
        upper = (tau & c) != 0
        qt = (q * jnp.exp(jnp.where(upper, b - prev(blk_end, c), MASK_NEG))).astype(BF16)
        kt = (kk * jnp.exp(jnp.where(upper, MASK_NEG, blk_end - b))).astype(BF16)
        same_parent = (ri >> (level + 1)) == (ci >> (level + 1))
        for hh in range(A_HEADS):
            sl = slice(hh * HD, (hh + 1) * HD)
            att[hh] = att[hh] + jnp.where(same_parent, _dot_nt(qt[:, sl], kt[:, sl]), 0.0)
        blk_end = jnp.where(upper, blk_end, nxt(blk_end, c))

    qin = (q * jnp.exp(b)).astype(BF16)
    kend = (kk * jnp.exp(blk_end - b)).astype(BF16)
    vb = vz.astype(BF16)
    decay = jnp.exp(blk_end[0:1, :])
    outs = []
    for hh in range(A_HEADS):
        sl = slice(hh * HD, (hh + 1) * HD)
        st = st_ref[hh]
        o = _dot_nt(qin[:, sl], st.astype(BF16)) + _dot(att[hh].astype(BF16), vb[:, sl])
        st_ref[hh] = st * decay[:, sl] + _dot_tn(vb[:, sl], kend[:, sl])
        outs.append(o)
    return jnp.concatenate(outs, axis=1)


def _hgrn_kernel(qf_ref, ff_ref, vf_ref, qb_ref, fb_ref, vb_ref, lb_ref, of_ref, ob_ref, st_ref, *, chunk):
    @pl.when(pl.program_id(1) == 0)
    def _():
        st_ref[...] = jnp.zeros_like(st_ref)

    of_ref[0] = _hgrn_chunk(qf_ref[0], ff_ref[0], vf_ref[0], lb_ref[0:1, :], st_ref.at[0], False, chunk)
    ob_ref[0] = _hgrn_chunk(qb_ref[0], fb_ref[0], vb_ref[0], lb_ref[1:2, :], st_ref.at[1], True, chunk)


def _hgrn_call(hg, lower, ctx_len):
    bsz, s, _ = hg.shape
    chunk = HGRN_CHUNK
    n = s // chunk
    nctx = ctx_len // chunk
    w = A_WIDTH

    def rmap(i):
        return jnp.where(i < nctx, nctx - 1 - i, n + nctx - 1 - i)

    def fwd(col):
        return pl.BlockSpec((1, chunk, w), lambda b, i: (b, i, col))

    def bwd(col):
        return pl.BlockSpec((1, chunk, w), lambda b, i: (b, rmap(i), col))

    return pl.pallas_call(
        functools.partial(_hgrn_kernel, chunk=chunk),
        out_shape=[jax.ShapeDtypeStruct((bsz, s, w), F32)] * 2,
        grid=(bsz, n),
        in_specs=[fwd(0), fwd(1), fwd(3), bwd(0), bwd(2), bwd(3),
                  pl.BlockSpec((2, w), lambda b, i: (0, 0))],
        out_specs=[pl.BlockSpec((1, chunk, w), lambda b, i: (b, i, 0)),
                   pl.BlockSpec((1, chunk, w), lambda b, i: (b, rmap(i), 0))],
        scratch_shapes=[pltpu.VMEM((2, A_HEADS, HD, HD), F32)],
        compiler_params=_cparams(("parallel", "arbitrary")),
        name="hgrn2_scan",
    )(hg, hg, hg, hg, hg, hg, lower)


def _attn_kernel(kmax_ref, q_ref, k_ref, vt_ref, qn2_ref, o_ref, *, tq, ts, n_sub, n_ctx_sub, nctx_q, unroll,
                 bounded):
    qi = pl.program_id(2)
    w = C_GROUP * tq
    q = q_ref[0].reshape(w, HD)

    def keys(c0, nc):
        return k_ref[0, 0, pl.ds(pl.multiple_of(c0 * ts, ts), nc * ts), :]

    def values_dot(c0, nc, p):
        pb = p.astype(BF16)
        out = _dot(vt_ref[0, 0, c0], pb[0:ts])
        for c in range(1, nc):
            out = out + _dot(vt_ref[0, 0, c0 + c], pb[c * ts:(c + 1) * ts])
        return out

    if bounded:
        kmax = kmax_ref[pl.program_id(0) * C_KV_HEADS + pl.program_id(1)]
        qn2 = jnp.concatenate([qn2_ref[0, 0, g:g + 1, :] for g in range(C_GROUP)], axis=1)
        shift = jnp.sqrt(qn2) * kmax

        def sub_tile(c0, nc, carry):
            l8, acc = carry
            p = jnp.exp2(_dot_nt(keys(c0, nc), q) - shift)
            l8 = l8 + jnp.sum(p.reshape(nc * ts // 8, 8, w), axis=0)
            return l8, acc + values_dot(c0, nc, p)

        init = (jnp.zeros((8, w), F32), jnp.zeros((HD, w), F32))
    else:
        def sub_tile(c0, nc, carry):
            m_prev, l_prev, acc = carry
            s = _dot_nt(keys(c0, nc), q)
            m_new = jnp.maximum(m_prev, jnp.max(s, axis=0, keepdims=True))
            alpha = jnp.exp2(m_prev - m_new)
            p = jnp.exp2(s - m_new)
            l_new = alpha * l_prev + jnp.sum(p, axis=0, keepdims=True)
            return m_new, l_new, alpha * acc + values_dot(c0, nc, p)

        init = (jnp.full((1, w), -jnp.inf, F32), jnp.zeros((1, w), F32), jnp.zeros((HD, w), F32))

    def run(n):
        carry = init
        if n >= unroll:
            carry = lax.fori_loop(0, n // unroll, lambda i, c: sub_tile(i * unroll, unroll, c), carry)
        if n % unroll:
            carry = sub_tile((n // unroll) * unroll, n % unroll, carry)
        l_fin = jnp.sum(carry[-2], axis=0, keepdims=True)
        ot = carry[-1] / l_fin
        stacked = jnp.concatenate([ot[:, g * tq:(g + 1) * tq] for g in range(C_GROUP)], axis=0)
        o_ref[0] = stacked.T.astype(o_ref.dtype)

    @pl.when(qi < nctx_q)
    def _():
        run(n_ctx_sub)

    @pl.when(qi >= nctx_q)
    def _():
        run(n_sub)


def _attn_call(kmax, q, k, vt, qn2, ctx_len, bounded):
    bsz, _, s, _ = q.shape
    tq = ATTN_TQ
    ts = vt.shape[-1]
    assert ctx_len % ts == 0
    n_sub = s // ts
    grid_spec = pltpu.PrefetchScalarGridSpec(
        num_scalar_prefetch=1,
        grid=(bsz, C_KV_HEADS, s // tq),
        in_specs=[
            pl.BlockSpec((1, C_GROUP, tq, HD), lambda b, g, qi, km: (b, g, qi, 0)),
            pl.BlockSpec((1, 1, s, HD), lambda b, g, qi, km: (b, g, 0, 0)),
            pl.BlockSpec((1, 1, n_sub, HD, ts), lambda b, g, qi, km: (b, g, 0, 0, 0)),
            pl.BlockSpec((1, 1, C_GROUP, tq), lambda b, g, qi, km: (b, g, 0, qi)),
        ],
        out_specs=pl.BlockSpec((1, tq, C_GROUP * HD), lambda b, g, qi, km: (b, qi, g)),
    )
    return pl.pallas_call(
        functools.partial(_attn_kernel, tq=tq, ts=ts, n_sub=n_sub, n_ctx_sub=ctx_len // ts,
                          nctx_q=ctx_len // tq, unroll=ATTN_UNROLL, bounded=bounded),
        out_shape=jax.ShapeDtypeStruct((bsz, s, C_HEADS * HD), BF16),
        grid_spec=grid_spec,
        compiler_params=_cparams(("parallel", "parallel", "arbitrary")),
        name="gqa_attention_bounded" if bounded else "gqa_attention",
    )(kmax, q, k, vt, qn2)


def _attention(q, k, vt, qn2, kn2, ctx_len):
    bsz = q.shape[0]
    k2 = jnp.max(kn2, axis=(1, 2)).reshape(bsz, C_KV_HEADS, HD)[:, :, 0]
    q2 = jnp.max(qn2.reshape(bsz, C_KV_HEADS, -1), axis=-1)
    kmax = (jnp.sqrt(k2) * BOUND_SLACK).reshape(-1)
    small = jnp.max(jnp.sqrt(q2).reshape(-1) * kmax) <= BOUND_MAX
    return lax.cond(small,
                    functools.partial(_attn_call, ctx_len=ctx_len, bounded=True),
                    functools.partial(_attn_call, ctx_len=ctx_len, bounded=False),
                    kmax, q, k, vt, qn2)


def _route(logits_t, bias):
    scores = jax.nn.sigmoid(logits_t)
    sel = scores + bias
    rows = [sel[e:e + 1, :] for e in range(N_EXPERTS)]
    srow = [scores[e:e + 1, :] for e in range(N_EXPERTS)]
    best_val = None
    best_grp = None
    for g in range(N_GROUPS):
        a = rows[g * E_PER_GROUP:(g + 1) * E_PER_GROUP]
        m1 = a[0]
        i1 = jnp.zeros_like(a[0], dtype=jnp.int32)
        for j in range(1, E_PER_GROUP):
            take = a[j] > m1
            m1 = jnp.where(take, a[j], m1)
            i1 = jnp.where(take, j, i1)
        m2 = jnp.full_like(m1, -jnp.inf)
        for j in range(E_PER_GROUP):
            m2 = jnp.where(i1 == j, m2, jnp.maximum(m2, a[j]))
        gs = m1 + m2
        if g == 0:
            best_val, best_grp = gs, jnp.zeros_like(i1)
        else:
            take = gs > best_val
            best_val = jnp.where(take, gs, best_val)
            best_grp = jnp.where(take, g, best_grp)
    masked = [jnp.where(best_grp == (e // E_PER_GROUP), rows[e], NEG_BIG) for e in range(N_EXPERTS)]
    v1 = masked[0]
    e1 = jnp.zeros_like(best_grp)
    for e in range(1, N_EXPERTS):
        take = masked[e] > v1
        v1 = jnp.where(take, masked[e], v1)
        e1 = jnp.where(take, e, e1)
    v2 = jnp.full_like(v1, -jnp.inf)
    e2 = jnp.zeros_like(best_grp)
    for e in range(N_EXPERTS):
        take = jnp.logical_and(e1 != e, masked[e] > v2)
        v2 = jnp.where(take, masked[e], v2)
        e2 = jnp.where(take, e, e2)
    w1 = jnp.zeros_like(v1)
    w2 = jnp.zeros_like(v1)
    for e in range(N_EXPERTS):
        w1 = jnp.where(e1 == e, srow[e], w1)
        w2 = jnp.where(e2 == e, srow[e], w2)
    tot = w1 + w2
    return jnp.concatenate([e1, e2], axis=0), jnp.concatenate([w1 / tot, w2 / tot], axis=0)


def _out_kernel(x_ref, of_ref, ob_ref, g_ref, pc_ref, pp_ref, pn_ref, oc_ref, mod_ref, ng_ref, pw_ref,
                ps_ref, wo_ref, g2_ref, rw_ref, rb_ref, bda_ref,
                xo_ref, h2_ref, idx_ref, wgt_ref, *, tm, seg_tiles):
    j = pl.program_id(1)
    o = of_ref[0] + ob_ref[0]
    gz = g_ref[0]
    oa = o * lax.rsqrt(_seg_mean_sq(o, bda_ref[...]) + EPS) * ng_ref[...] * (gz * jax.nn.sigmoid(gz))
    nctx, ntot = seg_tiles
    first = jnp.logical_or(j == 0, j == nctx)
    last = jnp.logical_or(j == nctx - 1, j == ntot - 1)
    cur = pc_ref[0]
    ext = jnp.concatenate([jnp.where(first, 0.0, pp_ref[0]), cur, jnp.where(last, 0.0, pn_ref[0])], axis=0)
    n = tm + 2 * POOL_HALO
    s2 = ext + pltpu.roll(ext, 1, axis=0)
    s4 = pltpu.roll(s2, 1, axis=0) + pltpu.roll(s2, n - 1, axis=0)
    s8 = pltpu.roll(s4, 2, axis=0) + pltpu.roll(s4, n - 2, axis=0)
    s16 = pltpu.roll(s8, 4, axis=0) + pltpu.roll(s8, n - 4, axis=0)
    sums = [a[POOL_HALO:POOL_HALO + tm, :] for a in (s2, s4, s8, s16)]
    seg_start = jnp.where(j < nctx, 0, nctx) * tm
    seg_len = jnp.where(j < nctx, nctx, ntot - nctx) * tm
    t = j * tm - seg_start + lax.broadcasted_iota(jnp.int32, (tm, 1), 0)
    lane = lax.broadcasted_iota(jnp.int32, (tm, 256), 1)
    mean = jnp.zeros((tm, 256), F32)
    for gi, win in enumerate(POOL_WINDOWS):
        cnt = (jnp.minimum(t + win // 2, seg_len) - jnp.maximum(t - win // 2, 0)).astype(F32)
        mean = jnp.where((lane >> 6) == gi, sums[gi] / cnt, mean)
    ob = _dot((mean - cur).astype(BF16), pw_ref[...]) * ps_ref[...]
    wo = wo_ref
    mix = (_dot(oa.astype(BF16), wo[0:256, :]) + _dot(ob.astype(BF16), wo[256:512, :])
           + _dot(oc_ref[0], wo[512:1024, :]))
    x = x_ref[0] + mod_ref[0, 0, 2:3, :] * mix
    xo_ref[0] = x
    ms = jnp.mean(x * x, axis=-1, keepdims=True)
    h2 = x * lax.rsqrt(ms + EPS) * g2_ref[...]
    h2 = h2 * (1.0 + mod_ref[0, 0, 4:5, :]) + mod_ref[0, 0, 3:4, :]
    for c in range(MOE_NC):
        h2_ref[c] = h2[:, c * MOE_CW:(c + 1) * MOE_CW]
    hhi, hlo = _split_bf16(h2)
    rhi, rlo = _split_bf16(rw_ref[...])
    logits_t = _dot_nt(rhi, hhi) + _dot_nt(rhi, hlo) + _dot_nt(rlo, hhi)
    ids, wts = _route(logits_t, rb_ref[...])
    idx_ref[0] = ids
    wgt_ref[0] = jnp.concatenate([wts, jnp.zeros((128 - 2, tm), F32)], axis=0).T


def _out_call(xs, of, ob, hg, pool, oc, mods, ng, pw, ps, wo, g2, rw_t, rb, bda, ctx_len):
    bsz, s, d = xs.shape
    tm = TOKEN_TILE
    nt = s // tm
    nctx = ctx_len // tm
    hb = tm // POOL_HALO
    nhb = s // POOL_HALO

    def tok(b, j):
        return (b, j, 0)

    def modmap(b, j):
        return (0, jnp.where(j < nctx, bsz, b), 0, 0)

    const2 = lambda b, j: (0, 0)
    return pl.pallas_call(
        functools.partial(_out_kernel, tm=tm, seg_tiles=(nctx, nt)),
        out_shape=[
            jax.ShapeDtypeStruct((bsz, s, d), F32),
            jax.ShapeDtypeStruct((MOE_NC, bsz * s, MOE_CW), F32),
            jax.ShapeDtypeStruct((bsz, 2, s), jnp.int32),
            jax.ShapeDtypeStruct((bsz, s, 128), F32),
        ],
        grid=(bsz, nt),
        in_specs=[
            pl.BlockSpec((1, tm, d), tok),
            pl.BlockSpec((1, tm, A_WIDTH), tok),
            pl.BlockSpec((1, tm, A_WIDTH), tok),
            pl.BlockSpec((1, tm, A_WIDTH), lambda b, j: (b, j, 4)),
            pl.BlockSpec((1, tm, 256), tok),
            pl.BlockSpec((1, POOL_HALO, 256), lambda b, j: (b, jnp.maximum(j * hb - 1, 0), 0)),
            pl.BlockSpec((1, POOL_HALO, 256), lambda b, j: (b, jnp.minimum((j + 1) * hb, nhb - 1), 0)),
            pl.BlockSpec((1, tm, C_HEADS * HD), tok),
            pl.BlockSpec((1, 1, 8, d), modmap),
            pl.BlockSpec((1, A_WIDTH), const2),
            pl.BlockSpec((256, 256), const2),
            pl.BlockSpec((1, 256), const2),
            pl.BlockSpec((d, d), const2),
            pl.BlockSpec((1, d), const2),
            pl.BlockSpec((N_EXPERTS, d), const2),
            pl.BlockSpec((N_EXPERTS, 1), const2),
            pl.BlockSpec((A_WIDTH, A_WIDTH), const2),
        ],
        out_specs=[
            pl.BlockSpec((1, tm, d), tok),
            pl.BlockSpec((MOE_NC, tm, MOE_CW), lambda b, j: (0, b * nt + j, 0)),
            pl.BlockSpec((1, 2, tm), lambda b, j: (b, 0, j)),
            pl.BlockSpec((1, tm, 128), tok),
        ],
        compiler_params=_cparams(("parallel", "arbitrary")),
        name="mix_out_proj_router",
    )(xs, of, ob, hg, pool, pool, pool, oc, mods, ng, pw, ps, wo, g2, rw_t, rb, bda)


def _moe_kernel(be_ref, nb_ref, x_ref, wg_ref, wu_ref, wd_ref, y_ref, wgb, wub, wdb):
    i = pl.program_id(0)
    prev = be_ref[jnp.maximum(i - 1, 0)]
    fresh = jnp.logical_or(i == 0, be_ref[i] != prev)

    @pl.when(fresh)
    def _():
        wgb[...] = wg_ref[0].astype(BF16)
        wub[...] = wu_ref[0].astype(BF16)
        wdb[...] = wd_ref[0].astype(BF16)

    @pl.when(i < nb_ref[0])
    def _():
        x = jnp.concatenate([x_ref[c] for c in range(MOE_NC)], axis=1).astype(BF16)
        gate = _dot(x, wgb[...])
        up = _dot(x, wub[...])
        a = gate * jax.nn.sigmoid(gate) * up
        y = _dot(a.astype(BF16), wdb[...])
        for c in range(MOE_NC):
            y_ref[c] = y[:, c * MOE_CW:(c + 1) * MOE_CW]

    @pl.when(i >= nb_ref[0])
    def _():
        y_ref[...] = jnp.zeros_like(y_ref)


def _moe_call(block_expert, nblocks, xg, wg, wu, wd):
    nc, p, cw = xg.shape
    _, d, f = wg.shape
    bm = MOE_BM
    grid_spec = pltpu.PrefetchScalarGridSpec(
        num_scalar_prefetch=2,
        grid=(p // bm,),
        in_specs=[
            pl.BlockSpec((nc, bm, cw), lambda i, be, nb: (0, i, 0)),
            pl.BlockSpec((1, d, f), lambda i, be, nb: (be[i], 0, 0)),
            pl.BlockSpec((1, d, f), lambda i, be, nb: (be[i], 0, 0)),
            pl.BlockSpec((1, f, d), lambda i, be, nb: (be[i], 0, 0)),
        ],
        out_specs=pl.BlockSpec((nc, bm, cw), lambda i, be, nb: (0, i, 0)),
        scratch_shapes=[pltpu.VMEM((d, f), BF16), pltpu.VMEM((d, f), BF16), pltpu.VMEM((f, d), BF16)],
    )
    return pl.pallas_call(
        _moe_kernel,
        out_shape=jax.ShapeDtypeStruct((nc, p, cw), F32),
        grid_spec=grid_spec,
        compiler_params=_cparams(("arbitrary",)),
        name="moe_experts",
    )(block_expert, nblocks, xg, wg, wu, wd)


def _sc_mesh():
    return plsc.VectorSubcoreMesh(core_axis_name="core", subcore_axis_name="subcore")


def _sc_scatter_rows(x, idx, n_out, src_block):
    m = idx.shape[0]

    @pl.kernel(out_type=jax.ShapeDtypeStruct((n_out, x.shape[1]), x.dtype), mesh=_sc_mesh())
    def scatter(x_hbm, i_hbm, o_hbm):
        def body(x_vmem, i_vmem):
            pltpu.sync_copy(x_vmem, o_hbm.at[i_vmem.at[0]])

        pltpu.emit_pipeline(
            body,
            grid=(m // SC_WINDOW,),
            in_specs=[pl.BlockSpec((SC_WINDOW, x.shape[1]), lambda w: (src_block(w), 0)),
                      pl.BlockSpec((1, SC_WINDOW), lambda w: (0, w))],
            out_specs=[],
            core_axis_name=("core", "subcore"),
            dimension_semantics=(pltpu.PARALLEL,),
        )(x_hbm, i_hbm)

    return scatter(x, idx.reshape(1, m))


def _sc_gather_rows(x, idx):
    m = idx.shape[0]

    @pl.kernel(out_type=jax.ShapeDtypeStruct((m, x.shape[1]), x.dtype), mesh=_sc_mesh())
    def gather(x_hbm, i_hbm, o_hbm):
        def body(i_vmem, o_vmem):
            pltpu.sync_copy(x_hbm.at[i_vmem.at[0]], o_vmem)

        pltpu.emit_pipeline(
            body,
            grid=(m // SC_WINDOW,),
            in_specs=[pl.BlockSpec((1, SC_WINDOW), lambda w: (0, w))],
            out_specs=[pl.BlockSpec((SC_WINDOW, x.shape[1]), lambda w: (w, 0))],
            core_axis_name=("core", "subcore"),
            dimension_semantics=(pltpu.PARALLEL,),
        )(i_hbm, o_hbm)

    return gather(x, idx.reshape(1, m))


def _moe_plan(idx, n_tokens):
    bm = MOE_BM
    flat = idx.reshape(-1)
    onehot = (flat[:, None] == jnp.arange(N_EXPERTS)[None, :]).astype(jnp.int32)
    rank = jnp.cumsum(onehot, axis=0) - onehot
    counts = jnp.sum(onehot, axis=0)
    padded = ((counts + bm - 1) // bm) * bm
    seg_end = jnp.cumsum(padded)
    seg_start = seg_end - padded
    slot = jnp.sum(onehot * (seg_start[None, :] + rank), axis=1)
    n_assign = flat.shape[0]
    p = (n_assign // bm + N_EXPERTS) * bm
    nb = p // bm
    blk_start = jnp.arange(nb) * bm
    block_expert = jnp.sum((blk_start[:, None] >= seg_end[None, :]).astype(jnp.int32), axis=1)
    nblocks = (seg_end[-1] // bm).astype(jnp.int32)
    last_e = jnp.max(jnp.where(counts > 0, jnp.arange(N_EXPERTS), 0))
    block_expert = jnp.minimum(block_expert, last_e).astype(jnp.int32)
    return slot, block_expert, nblocks.reshape(1), p


def _moe_combine(y_ref, w_ref):
    w0 = w_ref[0, :, 0:1]
    w1 = w_ref[0, :, 1:2]
    return jnp.concatenate([w0 * y_ref[c, 0, 0] + w1 * y_ref[c, 0, 1] for c in range(MOE_NC)], axis=1)


def _final_kernel(x_ref, y_ref, w_ref, mod_ref, o_ref):
    o_ref[0] = x_ref[0] + mod_ref[0, 0, 5:6, :] * _moe_combine(y_ref, w_ref)


def _final_call(xs, yc, wcol, mods, ctx_len):
    bsz, s, d = xs.shape
    tm = TOKEN_TILE
    nctx = ctx_len // tm
    t = s - ctx_len
    return pl.pallas_call(
        _final_kernel,
        out_shape=jax.ShapeDtypeStruct((bsz, t, d), F32),
        grid=(bsz, t // tm),
        in_specs=[
            pl.BlockSpec((1, tm, d), lambda b, j: (b, j + nctx, 0)),
            pl.BlockSpec((MOE_NC, 1, 2, tm, MOE_CW), lambda b, j: (0, b, 0, j + nctx, 0)),
            pl.BlockSpec((1, tm, 128), lambda b, j: (b, j + nctx, 0)),
            pl.BlockSpec((1, 1, 8, d), lambda b, j: (0, b, 0, 0)),
        ],
        out_specs=pl.BlockSpec((1, tm, d), lambda b, j: (b, j, 0)),
        compiler_params=_cparams(("parallel", "arbitrary")),
        name="final_residual",
    )(xs, yc, wcol, mods)


def _block_diag_ones(width):
    i = np.arange(width) // HD
    return jnp.asarray((i[:, None] == i[None, :]).astype(np.float32), dtype=BF16)


def _rope_tables(t, ctx_len):
    rows_n = t // GRID_W
    row = jnp.repeat(jnp.arange(rows_n), GRID_W).astype(F32)
    col = jnp.tile(jnp.arange(GRID_W), rows_n).astype(F32)
    n_freq = HD // 4
    inv = ROPE_THETA ** (-jnp.arange(n_freq, dtype=F32) / n_freq)
    ang = jnp.concatenate([row[:, None] * inv, col[:, None] * inv], axis=-1)
    cos, sin = jnp.cos(ang), jnp.sin(ang)
    cos_h = jnp.concatenate([cos, cos], axis=-1)
    sin_h = jnp.concatenate([-sin, sin], axis=-1)
    cos_h = jnp.concatenate([jnp.ones((ctx_len, HD), F32), cos_h], axis=0)
    sin_h = jnp.concatenate([jnp.zeros((ctx_len, HD), F32), sin_h], axis=0)
    return jnp.tile(cos_h, (1, 2)), jnp.tile(sin_h, (1, 2))


def kernel(x, c, ctx, c_ctx, w_mod, b_mod, norm1_g, norm2_g, w_in, w_out, hgrn_lb, hgrn_norm_g,
           pool_w, pool_scale, q_norm_g, k_norm_g, router_w, router_b, moe_w_gate, moe_w_up, moe_w_down):
    bsz, t, d = x.shape
    ctx_len = ctx.shape[1]
    depth = w_mod.shape[0]
    s = ctx_len + t
    assert ctx_len % TOKEN_TILE == 0 and t % TOKEN_TILE == 0 and t % GRID_W == 0
    assert ctx_len % HGRN_CHUNK == 0 and ctx_len % ATTN_TQ == 0

    perm = np.concatenate([np.arange(0, HD, 2), np.arange(1, HD, 2)])
    na = 5 * A_WIDTH
    q0 = na + 256
    cols = np.arange(w_in.shape[-1])
    qk_cols = cols[q0:q0 + (C_HEADS + C_KV_HEADS) * HD].reshape(-1, HD)[:, perm].reshape(-1)
    cols = np.concatenate([cols[:q0], qk_cols, cols[q0 + qk_cols.size:]])
    w_in_b = w_in[:, :, cols].astype(BF16)
    w_out_b = w_out.astype(BF16)
    gq = jnp.tile(q_norm_g[:, perm], (1, C_HEADS)).reshape(depth, 1, C_HEADS * HD)
    gk = jnp.tile(k_norm_g[:, perm], (1, C_KV_HEADS)).reshape(depth, 1, C_KV_HEADS * HD)
    ng = jnp.tile(hgrn_norm_g, (1, A_HEADS)).reshape(depth, 1, A_WIDTH)
    gw = 256 // B_GROUPS
    eye = jnp.eye(B_GROUPS, dtype=F32)
    pw_bd = (eye[None, :, None, :, None] * pool_w[:, :, :, None, :]).reshape(depth, 256, 256).astype(BF16)
    sm = jax.nn.softmax(hgrn_lb.astype(F32), axis=0)
    lower = jnp.cumsum(sm, axis=0) - sm[0:1]
    cos_t, sin_t = _rope_tables(t, ctx_len)
    bdq, bdk, bda = _block_diag_ones(C_HEADS * HD), _block_diag_ones(C_KV_HEADS * HD), _block_diag_ones(A_WIDTH)
    rw_t = router_w.T
    rb = router_b.reshape(N_EXPERTS, 1).astype(F32)

    cond = jnp.zeros((8, d), F32).at[:bsz].set(c).at[bsz].set(c_ctx)
    mods = _mod_call(cond, w_mod, b_mod)
    mods = mods[:, :bsz + 1].reshape(depth, bsz + 1, 6, d)
    mods = jnp.pad(mods, ((0, 0), (0, 0), (0, 2), (0, 0)))

    xs = jnp.concatenate([ctx, x], axis=1)
    moe = None
    n = bsz * s
    wps = s // SC_WINDOW

    def src_block(w):
        j = w % wps
        bk = w // wps
        return (bk // (2 * bsz)) * (n // SC_WINDOW) + ((bk // 2) % bsz) * wps + j

    for l in range(depth):
        outs = _in_call(xs, moe, mods[l - 1:l] if l else None, mods[l:l + 1], norm1_g[l:l + 1], w_in_b[l],
                        cos_t, sin_t, gq[l], gk[l], bdq, bdk, ctx_len)
        if moe is not None:
            xs, outs = outs[0], outs[1:]
        hg, pool, qh, kh, vh, qn2, kn2 = outs
        of, ob = _hgrn_call(hg, lower[l], ctx_len)
        oc = _attention(qh, kh, vh, qn2, kn2, ctx_len)
        xs, h2c, idx, wcol = _out_call(xs, of, ob, hg, pool, oc, mods[l:l + 1], ng[l], pw_bd[l],
                                       pool_scale[l:l + 1], w_out_b[l], norm2_g[l:l + 1], rw_t, rb, bda, ctx_len)
        slot, block_expert, nblocks, p = _moe_plan(idx, n)
        rows = (jnp.arange(MOE_NC, dtype=jnp.int32)[:, None] * p + slot[None, :]).reshape(-1)
        xg = _sc_scatter_rows(h2c.reshape(MOE_NC * n, MOE_CW), rows, MOE_NC * p, src_block)
        ys = _moe_call(block_expert, nblocks, xg.reshape(MOE_NC, p, MOE_CW),
                       moe_w_gate[l], moe_w_up[l], moe_w_down[l])
        yc = _sc_gather_rows(ys.reshape(MOE_NC * p, MOE_CW), rows)
        moe = (yc.reshape(MOE_NC, bsz, 2, s, MOE_CW), wcol)
    return _final_call(xs, moe[0], moe[1], mods[depth - 1:depth], ctx_len)
```

```python
import functools
import math

import jax
import jax.numpy as jnp
import numpy as np
from jax import lax
from jax.experimental import pallas as pl
from jax.experimental.pallas import tpu as pltpu
from jax.experimental.pallas import tpu_sc as plsc

F32 = jnp.float32
BF16 = jnp.bfloat16

EPS = 1e-6
LB_FLOOR = 1e-30
NEG_BIG = -1e4
GRID_W = 64
ROPE_THETA = 10000.0
HD = 64
A_HEADS = 4
A_WIDTH = A_HEADS * HD
B_GROUPS = 4
POOL_WINDOWS = (2, 4, 8, 16)
POOL_HALO = 8
C_HEADS = 8
C_KV_HEADS = 2
C_GROUP = C_HEADS // C_KV_HEADS
N_EXPERTS = 16
N_GROUPS = 4
E_PER_GROUP = N_EXPERTS // N_GROUPS
MASK_NEG = -1e30

TOKEN_TILE = 256
HGRN_CHUNK = 128
ATTN_TQ = 256
ATTN_UNROLL = 33
LOG2E = 1.4426950408889634
BOUND_SLACK = 1.02
BOUND_MAX = 48.0
MOE_BM = 256
MOE_NC = 4
MOE_CW = 256
SC_WINDOW = 128
VMEM_LIMIT = 56 * 1024 * 1024


def _cparams(sem):
    return pltpu.CompilerParams(dimension_semantics=sem, vmem_limit_bytes=VMEM_LIMIT)


def _dot(a, b):
    return jnp.dot(a, b, preferred_element_type=F32)


def _dot_nt(a, b):
    return lax.dot_general(a, b, (((1,), (1,)), ((), ())), preferred_element_type=F32)


def _dot_tn(a, b):
    return lax.dot_general(a, b, (((0,), (0,)), ((), ())), preferred_element_type=F32)


def _split_bf16(x):
    hi = x.astype(BF16)
    lo = (x - hi.astype(F32)).astype(BF16)
    return hi, lo


def _seg_mean_sq(x, bd):
    hi, lo = _split_bf16(x * x)
    return (_dot(hi, bd) + _dot(lo, bd)) * (1.0 / HD)


def _swap_halves(x):
    w = x.shape[-1]
    lane = lax.broadcasted_iota(jnp.int32, x.shape, x.ndim - 1)
    up = pltpu.roll(x, w - HD // 2, axis=x.ndim - 1)
    dn = pltpu.roll(x, HD // 2, axis=x.ndim - 1)
    return jnp.where((lane & (HD // 2)) == 0, up, dn)


def _mod_kernel(c_ref, w_ref, b_ref, o_ref):
    c = c_ref[...]
    a = c * jax.nn.sigmoid(c)
    hi, lo = _split_bf16(a)
    whi, wlo = _split_bf16(w_ref[0])
    o_ref[0] = _dot(hi, whi) + _dot(lo, whi) + _dot(hi, wlo) + b_ref[0]


def _mod_call(cond, w_mod, b_mod):
    depth, d, n6 = w_mod.shape
    tn = 1536
    return pl.pallas_call(
        _mod_kernel,
        out_shape=jax.ShapeDtypeStruct((depth, 8, n6), F32),
        grid=(depth, n6 // tn),
        in_specs=[
            pl.BlockSpec((8, d), lambda l, j: (0, 0)),
            pl.BlockSpec((1, d, tn), lambda l, j: (l, 0, j)),
            pl.BlockSpec((1, 1, tn), lambda l, j: (l, 0, j)),
        ],
        out_specs=pl.BlockSpec((1, 8, tn), lambda l, j: (l, 0, j)),
        compiler_params=_cparams(("arbitrary", "arbitrary")),
        name="adaln_mod",
    )(cond, w_mod, b_mod.reshape(depth, 1, n6))


def _in_kernel(*refs, has_moe):
    if has_moe:
        (x_ref, y_ref, wc_ref, modp_ref, mod_ref, g_ref, w_ref, cos_ref, sin_ref, gq_ref, gk_ref,
         bdq_ref, bdk_ref, selq_ref, xo_ref, hg_ref, pool_ref, q_ref, k_ref, v_ref, qn2_ref, kn2_ref) = refs
        x = x_ref[0] + modp_ref[0, 0, 5:6, :] * _moe_combine(y_ref, wc_ref)
        xo_ref[0] = x
    else:
        (x_ref, mod_ref, g_ref, w_ref, cos_ref, sin_ref, gq_ref, gk_ref,
         bdq_ref, bdk_ref, selq_ref, hg_ref, pool_ref, q_ref, k_ref, v_ref, qn2_ref, kn2_ref) = refs
        x = x_ref[0]
    ms = jnp.mean(x * x, axis=-1, keepdims=True)
    h = x * lax.rsqrt(ms + EPS) * g_ref[...]
    h = h * (1.0 + mod_ref[0, 0, 1:2, :]) + mod_ref[0, 0, 0:1, :]
    y = _dot(h.astype(BF16), w_ref[0])
    na = 5 * A_WIDTH
    hg_ref[0] = y[:, :na]
    pool_ref[0] = y[:, na:na + 256]
    q0 = na + 256
    qw = C_HEADS * HD
    kw = C_KV_HEADS * HD
    cos = cos_ref[...]
    sin = sin_ref[...]
    q = y[:, q0:q0 + qw]
    qn = q * lax.rsqrt(_seg_mean_sq(q, bdq_ref[...]) + EPS) * gq_ref[...]
    cos_q = jnp.concatenate([cos] * (qw // 128), axis=1)
    sin_q = jnp.concatenate([sin] * (qw // 128), axis=1)
    qr = (qn * cos_q + _swap_halves(qn) * sin_q) * (HD ** -0.5 * LOG2E)
    for hh in range(C_HEADS):
        q_ref[0, hh] = qr[:, hh * HD:(hh + 1) * HD].astype(BF16)
    k = y[:, q0 + qw:q0 + qw + kw]
    kn = k * lax.rsqrt(_seg_mean_sq(k, bdk_ref[...]) + EPS) * gk_ref[...]
    kr = kn * cos + _swap_halves(kn) * sin
    v = y[:, q0 + qw + kw:q0 + qw + 2 * kw]
    vt = v.T.astype(BF16)
    for hh in range(C_KV_HEADS):
        k_ref[0, hh] = kr[:, hh * HD:(hh + 1) * HD].astype(BF16)
        v_ref[0, hh, 0] = vt[hh * HD:(hh + 1) * HD, :]
    q2hi, q2lo = _split_bf16(qr * qr)
    qn2 = _dot_nt(selq_ref[...], q2hi) + _dot_nt(selq_ref[...], q2lo)
    for hh in range(C_KV_HEADS):
        qn2_ref[0, hh] = qn2[hh * C_GROUP:(hh + 1) * C_GROUP, :]
    k2hi, k2lo = _split_bf16(kr * kr)
    kn2 = _dot(k2hi, bdk_ref[...]) + _dot(k2lo, bdk_ref[...])
    kn2_ref[0, 0] = jnp.max(kn2, axis=0, keepdims=True)


def _in_call(xs, moe, mods_prev, mods, g1, w_in, layer, cos_t, sin_t, gq, gk, bdq, bdk, ctx_len):
    bsz, s, d = xs.shape
    tm = TOKEN_TILE
    nctx = ctx_len // tm
    ncols = w_in.shape[2]
    has_moe = moe is not None

    def tok(b, j):
        return (b, j, 0)

    def modmap(b, j):
        return (0, jnp.where(j < nctx, bsz, b), 0, 0)

    const2 = lambda b, j: (0, 0)
    in_specs = [pl.BlockSpec((1, tm, d), tok)]
    args = [xs]
    if has_moe:
        in_specs += [pl.BlockSpec((MOE_NC, 1, 2, tm, MOE_CW), lambda b, j: (0, b, 0, j, 0)),
                     pl.BlockSpec((1, tm, 128), tok), pl.BlockSpec((1, 1, 8, d), modmap)]
        args += [moe[0], moe[1], mods_prev]
    in_specs += [
        pl.BlockSpec((1, 1, 8, d), modmap),
        pl.BlockSpec((1, d), const2),
        pl.BlockSpec((1, d, ncols), lambda b, j: (layer, 0, 0)),
        pl.BlockSpec((tm, 128), lambda b, j: (j, 0)),
        pl.BlockSpec((tm, 128), lambda b, j: (j, 0)),
        pl.BlockSpec((1, C_HEADS * HD), const2),
        pl.BlockSpec((1, C_KV_HEADS * HD), const2),
        pl.BlockSpec((C_HEADS * HD, C_HEADS * HD), const2),
        pl.BlockSpec((C_KV_HEADS * HD, C_KV_HEADS * HD), const2),
        pl.BlockSpec((C_HEADS, C_HEADS * HD), const2),
    ]
    selq = bdq[::HD]
    args += [mods, g1, w_in, cos_t, sin_t, gq, gk, bdq, bdk, selq]
    out_shape = []
    out_specs = []
    if has_moe:
        out_shape.append(jax.ShapeDtypeStruct((bsz, s, d), F32))
        out_specs.append(pl.BlockSpec((1, tm, d), tok))
    out_shape += [
        jax.ShapeDtypeStruct((bsz, s, 5 * A_WIDTH), F32),
        jax.ShapeDtypeStruct((bsz, s, 256), F32),
        jax.ShapeDtypeStruct((bsz, C_HEADS, s, HD), BF16),
        jax.ShapeDtypeStruct((bsz, C_KV_HEADS, s, HD), BF16),
        jax.ShapeDtypeStruct((bsz, C_KV_HEADS, s // tm, HD, tm), BF16),
        jax.ShapeDtypeStruct((bsz, C_KV_HEADS, C_GROUP, s), F32),
        jax.ShapeDtypeStruct((bsz, s // tm, 1, C_KV_HEADS * HD), F32),
    ]
    out_specs += [
        pl.BlockSpec((1, tm, 5 * A_WIDTH), tok),
        pl.BlockSpec((1, tm, 256), tok),
        pl.BlockSpec((1, C_HEADS, tm, HD), lambda b, j: (b, 0, j, 0)),
        pl.BlockSpec((1, C_KV_HEADS, tm, HD), lambda b, j: (b, 0, j, 0)),
        pl.BlockSpec((1, C_KV_HEADS, 1, HD, tm), lambda b, j: (b, 0, j, 0, 0)),
        pl.BlockSpec((1, C_KV_HEADS, C_GROUP, tm), lambda b, j: (b, 0, 0, j)),
        pl.BlockSpec((1, 1, 1, C_KV_HEADS * HD), lambda b, j: (b, j, 0, 0)),
    ]
    return pl.pallas_call(
        functools.partial(_in_kernel, has_moe=has_moe),
        out_shape=out_shape,
        grid=(bsz, s // tm),
        in_specs=in_specs,
        out_specs=out_specs,
        compiler_params=_cparams(("parallel", "arbitrary")),
        name="norm_in_proj",
    )(*args)


def _hgrn_chunk(qz, fz, vz, lb, st_ref, rev, chunk):
    w = A_WIDTH
    row = lax.broadcasted_iota(jnp.int32, (chunk, w), 0)
    tau = (chunk - 1 - row) if rev else row

    def prev(x, k):
        return pltpu.roll(x, (chunk - k) if rev else k, axis=0)

    def nxt(x, k):
        return pltpu.roll(x, k if rev else (chunk - k), axis=0)

    q = qz * jax.nn.sigmoid(qz)
    lbf = jnp.maximum(lb, LB_FLOOR)
    e = jnp.exp(-jnp.abs(fz))
    r = 1.0 / (1.0 + e)
    er = e * r
    pos = fz >= 0
    f = lbf + (1.0 - lbf) * jnp.where(pos, r, er)
    kk = (1.0 - lbf) * jnp.where(pos, er, r)
    b = jnp.log(f)
    step = 1
    while step < chunk:
        b = b + jnp.where(tau >= step, prev(b, step), 0.0)
        step *= 2

    ri = lax.broadcasted_iota(jnp.int32, (chunk, chunk), 0)
    ci = lax.broadcasted_iota(jnp.int32, (chunk, chunk), 1)
    if rev:
        ri = chunk - 1 - ri
        ci = chunk - 1 - ci

    qb16 = q.astype(BF16)
    kb16 = kk.astype(BF16)
    att = []
    for hh in range(A_HEADS):
        sl = slice(hh * HD, (hh + 1) * HD)
        att.append(jnp.where(ri == ci, _dot_nt(qb16[:, sl], kb16[:, sl]), 0.0))

    blk_end = b
    for level in range(int(math.log2(chunk))):
        c = 1 << level
        upper = (tau & c) != 0
        qt = (q * jnp.exp(jnp.where(upper, b - prev(blk_end, c), MASK_NEG))).astype(BF16)
        kt = (kk * jnp.exp(jnp.where(upper, MASK_NEG, blk_end - b))).astype(BF16)
        same_parent = (ri >> (level + 1)) == (ci >> (level + 1))
        for hh in range(A_HEADS):
            sl = slice(hh * HD, (hh + 1) * HD)
            att[hh] = att[hh] + jnp.where(same_parent, _dot_nt(qt[:, sl], kt[:, sl]), 0.0)
        blk_end = jnp.where(upper, blk_end, nxt(blk_end, c))

    qin = (q * jnp.exp(b)).astype(BF16)
    kend = (kk * jnp.exp(blk_end - b)).astype(BF16)
    vb = vz.astype(BF16)
    decay = jnp.exp(blk_end[0:1, :])
    outs = []
    for hh in range(A_HEADS):
        sl = slice(hh * HD, (hh + 1) * HD)
        st = st_ref[hh]
        o = _dot_nt(qin[:, sl], st.astype(BF16)) + _dot(att[hh].astype(BF16), vb[:, sl])
        st_ref[hh] = st * decay[:, sl] + _dot_tn(vb[:, sl], kend[:, sl])
        outs.append(o)
    return jnp.concatenate(outs, axis=1)


def _hgrn_kernel(qf_ref, ff_ref, vf_ref, qb_ref, fb_ref, vb_ref, lb_ref, of_ref, ob_ref, st_ref, *, chunk):
    @pl.when(pl.program_id(1) == 0)
    def _():
        st_ref[...] = jnp.zeros_like(st_ref)

    of_ref[0] = _hgrn_chunk(qf_ref[0], ff_ref[0], vf_ref[0], lb_ref[0:1, :], st_ref.at[0], False, chunk)
    ob_ref[0] = _hgrn_chunk(qb_ref[0], fb_ref[0], vb_ref[0], lb_ref[1:2, :], st_ref.at[1], True, chunk)


def _hgrn_call(hg, lower, ctx_len):
    bsz, s, _ = hg.shape
    chunk = HGRN_CHUNK
    n = s // chunk
    nctx = ctx_len // chunk
    w = A_WIDTH

    def rmap(i):
        return jnp.where(i < nctx, nctx - 1 - i, n + nctx - 1 - i)

    def fwd(col):
        return pl.BlockSpec((1, chunk, w), lambda b, i: (b, i, col))

    def bwd(col):
        return pl.BlockSpec((1, chunk, w), lambda b, i: (b, rmap(i), col))

    return pl.pallas_call(
        functools.partial(_hgrn_kernel, chunk=chunk),
        out_shape=[jax.ShapeDtypeStruct((bsz, s, w), F32)] * 2,
        grid=(bsz, n),
        in_specs=[fwd(0), fwd(1), fwd(3), bwd(0), bwd(2), bwd(3),
                  pl.BlockSpec((2, w), lambda b, i: (0, 0))],
        out_specs=[pl.BlockSpec((1, chunk, w), lambda b, i: (b, i, 0)),
                   pl.BlockSpec((1, chunk, w), lambda b, i: (b, rmap(i), 0))],
        scratch_shapes=[pltpu.VMEM((2, A_HEADS, HD, HD), F32)],
        compiler_params=_cparams(("parallel", "arbitrary")),
        name="hgrn2_scan",
    )(hg, hg, hg, hg, hg, hg, lower)


def _attn_kernel(kmax_ref, q_ref, k_ref, vt_ref, qn2_ref, o_ref, *, tq, ts, n_sub, n_ctx_sub, nctx_q, unroll,
                 bounded):
    qi = pl.program_id(2)
    w = C_GROUP * tq
    q = q_ref[0].reshape(w, HD)

    def keys(c0, nc):
        return k_ref[0, 0, pl.ds(pl.multiple_of(c0 * ts, ts), nc * ts), :]

    def values_dot(c0, nc, p):
        pb = p.astype(BF16)
        out = _dot(vt_ref[0, 0, c0], pb[0:ts])
        for c in range(1, nc):
            out = out + _dot(vt_ref[0, 0, c0 + c], pb[c * ts:(c + 1) * ts])
        return out

    if bounded:
        kmax = kmax_ref[pl.program_id(0) * C_KV_HEADS + pl.program_id(1)]
        qn2 = jnp.concatenate([qn2_ref[0, 0, g:g + 1, :] for g in range(C_GROUP)], axis=1)
        shift = jnp.sqrt(qn2) * kmax

        def sub_tile(c0, nc, carry):
            l8, acc = carry
            p = jnp.exp2(_dot_nt(keys(c0, nc), q) - shift)
            l8 = l8 + jnp.sum(p.reshape(nc * ts // 8, 8, w), axis=0)
            return l8, acc + values_dot(c0, nc, p)

        init = (jnp.zeros((8, w), F32), jnp.zeros((HD, w), F32))
    else:
        def sub_tile(c0, nc, carry):
            m_prev, l_prev, acc = carry
            s = _dot_nt(keys(c0, nc), q)
            m_new = jnp.maximum(m_prev, jnp.max(s, axis=0, keepdims=True))
            alpha = jnp.exp2(m_prev - m_new)
            p = jnp.exp2(s - m_new)
            l_new = alpha * l_prev + jnp.sum(p, axis=0, keepdims=True)
            return m_new, l_new, alpha * acc + values_dot(c0, nc, p)

        init = (jnp.full((1, w), -jnp.inf, F32), jnp.zeros((1, w), F32), jnp.zeros((HD, w), F32))

    def run(n):
        carry = init
        if n >= unroll:
            carry = lax.fori_loop(0, n // unroll, lambda i, c: sub_tile(i * unroll, unroll, c), carry)
        if n % unroll:
            carry = sub_tile((n // unroll) * unroll, n % unroll, carry)
        l_fin = jnp.sum(carry[-2], axis=0, keepdims=True)
        ot = carry[-1] / l_fin
        stacked = jnp.concatenate([ot[:, g * tq:(g + 1) * tq] for g in range(C_GROUP)], axis=0)
        o_ref[0] = stacked.T.astype(o_ref.dtype)

    @pl.when(qi < nctx_q)
    def _():
        run(n_ctx_sub)

    @pl.when(qi >= nctx_q)
    def _():
        run(n_sub)


def _attn_call(kmax, q, k, vt, qn2, ctx_len, bounded):
    bsz, _, s, _ = q.shape
    tq = ATTN_TQ
    ts = vt.shape[-1]
    assert ctx_len % ts == 0
    n_sub = s // ts
    grid_spec = pltpu.PrefetchScalarGridSpec(
        num_scalar_prefetch=1,
        grid=(bsz, C_KV_HEADS, s // tq),
        in_specs=[
            pl.BlockSpec((1, C_GROUP, tq, HD), lambda b, g, qi, km: (b, g, qi, 0)),
            pl.BlockSpec((1, 1, s, HD), lambda b, g, qi, km: (b, g, 0, 0)),
            pl.BlockSpec((1, 1, n_sub, HD, ts), lambda b, g, qi, km: (b, g, 0, 0, 0)),
            pl.BlockSpec((1, 1, C_GROUP, tq), lambda b, g, qi, km: (b, g, 0, qi)),
        ],
        out_specs=pl.BlockSpec((1, tq, C_GROUP * HD), lambda b, g, qi, km: (b, qi, g)),
    )
    return pl.pallas_call(
        functools.partial(_attn_kernel, tq=tq, ts=ts, n_sub=n_sub, n_ctx_sub=ctx_len // ts,
                          nctx_q=ctx_len // tq, unroll=ATTN_UNROLL, bounded=bounded),
        out_shape=jax.ShapeDtypeStruct((bsz, s, C_HEADS * HD), BF16),
        grid_spec=grid_spec,
        compiler_params=_cparams(("parallel", "parallel", "arbitrary")),
        name="gqa_attention_bounded" if bounded else "gqa_attention",
    )(kmax, q, k, vt, qn2)


def _attention(q, k, vt, qn2, kn2, ctx_len):
    bsz = q.shape[0]
    k2 = jnp.max(kn2, axis=(1, 2)).reshape(bsz, C_KV_HEADS, HD)[:, :, 0]
    q2 = jnp.max(qn2.reshape(bsz, C_KV_HEADS, -1), axis=-1)
    kmax = (jnp.sqrt(k2) * BOUND_SLACK).reshape(-1)
    small = jnp.max(jnp.sqrt(q2).reshape(-1) * kmax) <= BOUND_MAX
    return lax.cond(small,
                    functools.partial(_attn_call, ctx_len=ctx_len, bounded=True),
                    functools.partial(_attn_call, ctx_len=ctx_len, bounded=False),
                    kmax, q, k, vt, qn2)


def _route(logits_t, bias):
    scores = jax.nn.sigmoid(logits_t)
    sel = scores + bias
    rows = [sel[e:e + 1, :] for e in range(N_EXPERTS)]
    srow = [scores[e:e + 1, :] for e in range(N_EXPERTS)]
    best_val = None
    best_grp = None
    for g in range(N_GROUPS):
        a = rows[g * E_PER_GROUP:(g + 1) * E_PER_GROUP]
        m1 = a[0]
        i1 = jnp.zeros_like(a[0], dtype=jnp.int32)
        for j in range(1, E_PER_GROUP):
            take = a[j] > m1
            m1 = jnp.where(take, a[j], m1)
            i1 = jnp.where(take, j, i1)
        m2 = jnp.full_like(m1, -jnp.inf)
        for j in range(E_PER_GROUP):
            m2 = jnp.where(i1 == j, m2, jnp.maximum(m2, a[j]))
        gs = m1 + m2
        if g == 0:
            best_val, best_grp = gs, jnp.zeros_like(i1)
        else:
            take = gs > best_val
            best_val = jnp.where(take, gs, best_val)
            best_grp = jnp.where(take, g, best_grp)
    masked = [jnp.where(best_grp == (e // E_PER_GROUP), rows[e], NEG_BIG) for e in range(N_EXPERTS)]
    v1 = masked[0]
    e1 = jnp.zeros_like(best_grp)
    for e in range(1, N_EXPERTS):
        take = masked[e] > v1
        v1 = jnp.where(take, masked[e], v1)
        e1 = jnp.where(take, e, e1)
    v2 = jnp.full_like(v1, -jnp.inf)
    e2 = jnp.zeros_like(best_grp)
    for e in range(N_EXPERTS):
        take = jnp.logical_and(e1 != e, masked[e] > v2)
        v2 = jnp.where(take, masked[e], v2)
        e2 = jnp.where(take, e, e2)
    w1 = jnp.zeros_like(v1)
    w2 = jnp.zeros_like(v1)
    for e in range(N_EXPERTS):
        w1 = jnp.where(e1 == e, srow[e], w1)
        w2 = jnp.where(e2 == e, srow[e], w2)
    tot = w1 + w2
    return jnp.concatenate([e1, e2], axis=0), jnp.concatenate([w1 / tot, w2 / tot], axis=0)


def _out_kernel(x_ref, of_ref, ob_ref, g_ref, pc_ref, pp_ref, pn_ref, oc_ref, mod_ref, ng_ref, pw_ref,
                ps_ref, wo_ref, g2_ref, rw_ref, rb_ref, bda_ref,
                xo_ref, h2_ref, idx_ref, wgt_ref, *, tm, seg_tiles):
    j = pl.program_id(1)
    o = of_ref[0] + ob_ref[0]
    gz = g_ref[0]
    oa = o * lax.rsqrt(_seg_mean_sq(o, bda_ref[...]) + EPS) * ng_ref[...] * (gz * jax.nn.sigmoid(gz))
    nctx, ntot = seg_tiles
    first = jnp.logical_or(j == 0, j == nctx)
    last = jnp.logical_or(j == nctx - 1, j == ntot - 1)
    cur = pc_ref[0]
    ext = jnp.concatenate([jnp.where(first, 0.0, pp_ref[0]), cur, jnp.where(last, 0.0, pn_ref[0])], axis=0)
    n = tm + 2 * POOL_HALO
    s2 = ext + pltpu.roll(ext, 1, axis=0)
    s4 = pltpu.roll(s2, 1, axis=0) + pltpu.roll(s2, n - 1, axis=0)
    s8 = pltpu.roll(s4, 2, axis=0) + pltpu.roll(s4, n - 2, axis=0)
    s16 = pltpu.roll(s8, 4, axis=0) + pltpu.roll(s8, n - 4, axis=0)
    sums = [a[POOL_HALO:POOL_HALO + tm, :] for a in (s2, s4, s8, s16)]
    seg_start = jnp.where(j < nctx, 0, nctx) * tm
    seg_len = jnp.where(j < nctx, nctx, ntot - nctx) * tm
    t = j * tm - seg_start + lax.broadcasted_iota(jnp.int32, (tm, 1), 0)
    lane = lax.broadcasted_iota(jnp.int32, (tm, 256), 1)
    mean = jnp.zeros((tm, 256), F32)
    for gi, win in enumerate(POOL_WINDOWS):
        cnt = (jnp.minimum(t + win // 2, seg_len) - jnp.maximum(t - win // 2, 0)).astype(F32)
        mean = jnp.where((lane >> 6) == gi, sums[gi] / cnt, mean)
    ob = _dot((mean - cur).astype(BF16), pw_ref[...]) * ps_ref[...]
    mix = (_dot(oa.astype(BF16), wo_ref[0, 0:256, :]) + _dot(ob.astype(BF16), wo_ref[0, 256:512, :])
           + _dot(oc_ref[0], wo_ref[0, 512:1024, :]))
    x = x_ref[0] + mod_ref[0, 0, 2:3, :] * mix
    xo_ref[0] = x
    ms = jnp.mean(x * x, axis=-1, keepdims=True)
    h2 = x * lax.rsqrt(ms + EPS) * g2_ref[...]
    h2 = h2 * (1.0 + mod_ref[0, 0, 4:5, :]) + mod_ref[0, 0, 3:4, :]
    for c in range(MOE_NC):
        h2_ref[c] = h2[:, c * MOE_CW:(c + 1) * MOE_CW]
    hhi, hlo = _split_bf16(h2)
    rhi, rlo = _split_bf16(rw_ref[...])
    logits_t = _dot_nt(rhi, hhi) + _dot_nt(rhi, hlo) + _dot_nt(rlo, hhi)
    ids, wts = _route(logits_t, rb_ref[...])
    idx_ref[0] = ids
    wgt_ref[0] = jnp.concatenate([wts, jnp.zeros((128 - 2, tm), F32)], axis=0).T


def _out_call(xs, of, ob, hg, pool, oc, mods, ng, pw, ps, wo, layer, g2, rw_t, rb, bda, ctx_len):
    bsz, s, d = xs.shape
    tm = TOKEN_TILE
    nt = s // tm
    nctx = ctx_len // tm
    hb = tm // POOL_HALO
    nhb = s // POOL_HALO

    def tok(b, j):
        return (b, j, 0)

    def modmap(b, j):
        return (0, jnp.where(j < nctx, bsz, b), 0, 0)

    const2 = lambda b, j: (0, 0)
    return pl.pallas_call(
        functools.partial(_out_kernel, tm=tm, seg_tiles=(nctx, nt)),
        out_shape=[
            jax.ShapeDtypeStruct((bsz, s, d), F32),
            jax.ShapeDtypeStruct((MOE_NC, bsz * s, MOE_CW), F32),
            jax.ShapeDtypeStruct((bsz, 2, s), jnp.int32),
            jax.ShapeDtypeStruct((bsz, s, 128), F32),
        ],
        grid=(bsz, nt),
        in_specs=[
            pl.BlockSpec((1, tm, d), tok),
            pl.BlockSpec((1, tm, A_WIDTH), tok),
            pl.BlockSpec((1, tm, A_WIDTH), tok),
            pl.BlockSpec((1, tm, A_WIDTH), lambda b, j: (b, j, 4)),
            pl.BlockSpec((1, tm, 256), tok),
            pl.BlockSpec((1, POOL_HALO, 256), lambda b, j: (b, jnp.maximum(j * hb - 1, 0), 0)),
            pl.BlockSpec((1, POOL_HALO, 256), lambda b, j: (b, jnp.minimum((j + 1) * hb, nhb - 1), 0)),
            pl.BlockSpec((1, tm, C_HEADS * HD), tok),
            pl.BlockSpec((1, 1, 8, d), modmap),
            pl.BlockSpec((1, A_WIDTH), const2),
            pl.BlockSpec((256, 256), const2),
            pl.BlockSpec((1, 256), const2),
            pl.BlockSpec((1, d, d), lambda b, j: (layer, 0, 0)),
            pl.BlockSpec((1, d), const2),
            pl.BlockSpec((N_EXPERTS, d), const2),
            pl.BlockSpec((N_EXPERTS, 1), const2),
            pl.BlockSpec((A_WIDTH, A_WIDTH), const2),
        ],
        out_specs=[
            pl.BlockSpec((1, tm, d), tok),
            pl.BlockSpec((MOE_NC, tm, MOE_CW), lambda b, j: (0, b * nt + j, 0)),
            pl.BlockSpec((1, 2, tm), lambda b, j: (b, 0, j)),
            pl.BlockSpec((1, tm, 128), tok),
        ],
        compiler_params=_cparams(("parallel", "arbitrary")),
        name="mix_out_proj_router",
    )(xs, of, ob, hg, pool, pool, pool, oc, mods, ng, pw, ps, wo, g2, rw_t, rb, bda)


def _moe_kernel(be_ref, nb_ref, x_ref, wg_ref, wu_ref, wd_ref, y_ref, wgb, wub, wdb):
    i = pl.program_id(0)
    prev = be_ref[jnp.maximum(i - 1, 0)]
    fresh = jnp.logical_or(i == 0, be_ref[i] != prev)

    @pl.when(fresh)
    def _():
        wgb[...] = wg_ref[0, 0].astype(BF16)
        wub[...] = wu_ref[0, 0].astype(BF16)
        wdb[...] = wd_ref[0, 0].astype(BF16)

    @pl.when(i < nb_ref[0])
    def _():
        x = jnp.concatenate([x_ref[c] for c in range(MOE_NC)], axis=1).astype(BF16)
        gate = _dot(x, wgb[...])
        up = _dot(x, wub[...])
        a = gate * jax.nn.sigmoid(gate) * up
        y = _dot(a.astype(BF16), wdb[...])
        for c in range(MOE_NC):
            y_ref[c] = y[:, c * MOE_CW:(c + 1) * MOE_CW]

    @pl.when(i >= nb_ref[0])
    def _():
        y_ref[...] = jnp.zeros_like(y_ref)


def _moe_call(block_expert, nblocks, xg, wg, wu, wd, layer):
    nc, p, cw = xg.shape
    _, _, d, f = wg.shape
    bm = MOE_BM
    grid_spec = pltpu.PrefetchScalarGridSpec(
        num_scalar_prefetch=2,
        grid=(p // bm,),
        in_specs=[
            pl.BlockSpec((nc, bm, cw), lambda i, be, nb: (0, i, 0)),
            pl.BlockSpec((1, 1, d, f), lambda i, be, nb: (layer, be[i], 0, 0)),
            pl.BlockSpec((1, 1, d, f), lambda i, be, nb: (layer, be[i], 0, 0)),
            pl.BlockSpec((1, 1, f, d), lambda i, be, nb: (layer, be[i], 0, 0)),
        ],
        out_specs=pl.BlockSpec((nc, bm, cw), lambda i, be, nb: (0, i, 0)),
        scratch_shapes=[pltpu.VMEM((d, f), BF16), pltpu.VMEM((d, f), BF16), pltpu.VMEM((f, d), BF16)],
    )
    return pl.pallas_call(
        _moe_kernel,
        out_shape=jax.ShapeDtypeStruct((nc, p, cw), F32),
        grid_spec=grid_spec,
        compiler_params=_cparams(("arbitrary",)),
        name="moe_experts",
    )(block_expert, nblocks, xg, wg, wu, wd)


def _sc_mesh():
    return plsc.VectorSubcoreMesh(core_axis_name="core", subcore_axis_name="subcore")


def _sc_scatter_rows(x, idx, n_out, src_block):
    m = idx.shape[0]

    @pl.kernel(out_type=jax.ShapeDtypeStruct((n_out, x.shape[1]), x.dtype), mesh=_sc_mesh())
    def scatter(x_hbm, i_hbm, o_hbm):
        def body(x_vmem, i_vmem):
            pltpu.sync_copy(x_vmem, o_hbm.at[i_vmem.at[0]])

        pltpu.emit_pipeline(
            body,
            grid=(m // SC_WINDOW,),
            in_specs=[pl.BlockSpec((SC_WINDOW, x.shape[1]), lambda w: (src_block(w), 0)),
                      pl.BlockSpec((1, SC_WINDOW), lambda w: (0, w))],
            out_specs=[],
            core_axis_name=("core", "subcore"),
            dimension_semantics=(pltpu.PARALLEL,),
        )(x_hbm, i_hbm)

    return scatter(x, idx.reshape(1, m))


def _sc_gather_rows(x, idx):
    m = idx.shape[0]

    @pl.kernel(out_type=jax.ShapeDtypeStruct((m, x.shape[1]), x.dtype), mesh=_sc_mesh())
    def gather(x_hbm, i_hbm, o_hbm):
        def body(i_vmem, o_vmem):
            pltpu.sync_copy(x_hbm.at[i_vmem.at[0]], o_vmem)

        pltpu.emit_pipeline(
            body,
            grid=(m // SC_WINDOW,),
            in_specs=[pl.BlockSpec((1, SC_WINDOW), lambda w: (0, w))],
            out_specs=[pl.BlockSpec((SC_WINDOW, x.shape[1]), lambda w: (w, 0))],
            core_axis_name=("core", "subcore"),
            dimension_semantics=(pltpu.PARALLEL,),
        )(i_hbm, o_hbm)

    return gather(x, idx.reshape(1, m))


def _moe_plan(idx, n_tokens):
    bm = MOE_BM
    flat = idx.reshape(-1)
    onehot = (flat[:, None] == jnp.arange(N_EXPERTS)[None, :]).astype(jnp.int32)
    rank = jnp.cumsum(onehot, axis=0) - onehot
    counts = jnp.sum(onehot, axis=0)
    padded = ((counts + bm - 1) // bm) * bm
    seg_end = jnp.cumsum(padded)
    seg_start = seg_end - padded
    slot = jnp.sum(onehot * (seg_start[None, :] + rank), axis=1)
    n_assign = flat.shape[0]
    p = (n_assign // bm + N_EXPERTS) * bm
    nb = p // bm
    blk_start = jnp.arange(nb) * bm
    block_expert = jnp.sum((blk_start[:, None] >= seg_end[None, :]).astype(jnp.int32), axis=1)
    nblocks = (seg_end[-1] // bm).astype(jnp.int32)
    last_e = jnp.max(jnp.where(counts > 0, jnp.arange(N_EXPERTS), 0))
    block_expert = jnp.minimum(block_expert, last_e).astype(jnp.int32)
    return slot, block_expert, nblocks.reshape(1), p


def _moe_combine(y_ref, w_ref):
    w0 = w_ref[0, :, 0:1]
    w1 = w_ref[0, :, 1:2]
    return jnp.concatenate([w0 * y_ref[c, 0, 0] + w1 * y_ref[c, 0, 1] for c in range(MOE_NC)], axis=1)


def _final_kernel(x_ref, y_ref, w_ref, mod_ref, o_ref):
    o_ref[0] = x_ref[0] + mod_ref[0, 0, 5:6, :] * _moe_combine(y_ref, w_ref)


def _final_call(xs, yc, wcol, mods, ctx_len):
    bsz, s, d = xs.shape
    tm = TOKEN_TILE
    nctx = ctx_len // tm
    t = s - ctx_len
    return pl.pallas_call(
        _final_kernel,
        out_shape=jax.ShapeDtypeStruct((bsz, t, d), F32),
        grid=(bsz, t // tm),
        in_specs=[
            pl.BlockSpec((1, tm, d), lambda b, j: (b, j + nctx, 0)),
            pl.BlockSpec((MOE_NC, 1, 2, tm, MOE_CW), lambda b, j: (0, b, 0, j + nctx, 0)),
            pl.BlockSpec((1, tm, 128), lambda b, j: (b, j + nctx, 0)),
            pl.BlockSpec((1, 1, 8, d), lambda b, j: (0, b, 0, 0)),
        ],
        out_specs=pl.BlockSpec((1, tm, d), lambda b, j: (b, j, 0)),
        compiler_params=_cparams(("parallel", "arbitrary")),
        name="final_residual",
    )(xs, yc, wcol, mods)


def _block_diag_ones(width):
    i = np.arange(width) // HD
    return jnp.asarray((i[:, None] == i[None, :]).astype(np.float32), dtype=BF16)


def _rope_tables(t, ctx_len):
    rows_n = t // GRID_W
    row = jnp.repeat(jnp.arange(rows_n), GRID_W).astype(F32)
    col = jnp.tile(jnp.arange(GRID_W), rows_n).astype(F32)
    n_freq = HD // 4
    inv = ROPE_THETA ** (-jnp.arange(n_freq, dtype=F32) / n_freq)
    ang = jnp.concatenate([row[:, None] * inv, col[:, None] * inv], axis=-1)
    cos, sin = jnp.cos(ang), jnp.sin(ang)
    cos_h = jnp.concatenate([cos, cos], axis=-1)
    sin_h = jnp.concatenate([-sin, sin], axis=-1)
    cos_h = jnp.concatenate([jnp.ones((ctx_len, HD), F32), cos_h], axis=0)
    sin_h = jnp.concatenate([jnp.zeros((ctx_len, HD), F32), sin_h], axis=0)
    return jnp.tile(cos_h, (1, 2)), jnp.tile(sin_h, (1, 2))


def kernel(x, c, ctx, c_ctx, w_mod, b_mod, norm1_g, norm2_g, w_in, w_out, hgrn_lb, hgrn_norm_g,
           pool_w, pool_scale, q_norm_g, k_norm_g, router_w, router_b, moe_w_gate, moe_w_up, moe_w_down):
    bsz, t, d = x.shape
    ctx_len = ctx.shape[1]
    depth = w_mod.shape[0]
    s = ctx_len + t
    assert ctx_len % TOKEN_TILE == 0 and t % TOKEN_TILE == 0 and t % GRID_W == 0
    assert ctx_len % HGRN_CHUNK == 0 and ctx_len % ATTN_TQ == 0

    perm = np.concatenate([np.arange(0, HD, 2), np.arange(1, HD, 2)])
    na = 5 * A_WIDTH
    q0 = na + 256
    nqk = (C_HEADS + C_KV_HEADS) * HD
    w_qk = w_in[:, :, q0:q0 + nqk].reshape(depth, d, nqk // HD, HD // 2, 2)
    w_qk = jnp.swapaxes(w_qk, -1, -2).reshape(depth, d, nqk)
    w_in_b = jnp.concatenate([w_in[:, :, :q0], w_qk, w_in[:, :, q0 + nqk:]], axis=-1).astype(BF16)
    w_out_b = w_out.astype(BF16)
    gq = jnp.tile(q_norm_g[:, perm], (1, C_HEADS)).reshape(depth, 1, C_HEADS * HD)
    gk = jnp.tile(k_norm_g[:, perm], (1, C_KV_HEADS)).reshape(depth, 1, C_KV_HEADS * HD)
    ng = jnp.tile(hgrn_norm_g, (1, A_HEADS)).reshape(depth, 1, A_WIDTH)
    gw = 256 // B_GROUPS
    eye = jnp.eye(B_GROUPS, dtype=F32)
    pw_bd = (eye[None, :, None, :, None] * pool_w[:, :, :, None, :]).reshape(depth, 256, 256).astype(BF16)
    sm = jax.nn.softmax(hgrn_lb.astype(F32), axis=0)
    lower = jnp.cumsum(sm, axis=0) - sm[0:1]
    cos_t, sin_t = _rope_tables(t, ctx_len)
    bdq, bdk, bda = _block_diag_ones(C_HEADS * HD), _block_diag_ones(C_KV_HEADS * HD), _block_diag_ones(A_WIDTH)
    rw_t = router_w.T
    rb = router_b.reshape(N_EXPERTS, 1).astype(F32)

    cond = jnp.zeros((8, d), F32).at[:bsz].set(c).at[bsz].set(c_ctx)
    mods = _mod_call(cond, w_mod, b_mod)
    mods = mods[:, :bsz + 1].reshape(depth, bsz + 1, 6, d)
    mods = jnp.pad(mods, ((0, 0), (0, 0), (0, 2), (0, 0)))

    xs = jnp.concatenate([ctx, x], axis=1)
    moe = None
    n = bsz * s
    wps = s // SC_WINDOW

    def src_block(w):
        j = w % wps
        bk = w // wps
        return (bk // (2 * bsz)) * (n // SC_WINDOW) + ((bk // 2) % bsz) * wps + j

    for l in range(depth):
        outs = _in_call(xs, moe, mods[l - 1:l] if l else None, mods[l:l + 1], norm1_g[l:l + 1], w_in_b, l,
                        cos_t, sin_t, gq[l], gk[l], bdq, bdk, ctx_len)
        if moe is not None:
            xs, outs = outs[0], outs[1:]
        hg, pool, qh, kh, vh, qn2, kn2 = outs
        of, ob = _hgrn_call(hg, lower[l], ctx_len)
        oc = _attention(qh, kh, vh, qn2, kn2, ctx_len)
        xs, h2c, idx, wcol = _out_call(xs, of, ob, hg, pool, oc, mods[l:l + 1], ng[l], pw_bd[l],
                                       pool_scale[l:l + 1], w_out_b, l, norm2_g[l:l + 1], rw_t, rb, bda, ctx_len)
        slot, block_expert, nblocks, p = _moe_plan(idx, n)
        rows = (jnp.arange(MOE_NC, dtype=jnp.int32)[:, None] * p + slot[None, :]).reshape(-1)
        xg = _sc_scatter_rows(h2c.reshape(MOE_NC * n, MOE_CW), rows, MOE_NC * p, src_block)
        ys = _moe_call(block_expert, nblocks, xg.reshape(MOE_NC, p, MOE_CW),
                       moe_w_gate, moe_w_up, moe_w_down, l)
        yc = _sc_gather_rows(ys.reshape(MOE_NC * p, MOE_CW), rows)
        moe = (yc.reshape(MOE_NC, bsz, 2, s, MOE_CW), wcol)
    return _final_call(xs, moe[0], moe[1], mods[depth - 1:depth], ctx_len)
```

```python
import functools
import math

import jax
import jax.numpy as jnp
import numpy as np
from jax import lax
from jax.experimental import pallas as pl
from jax.experimental.pallas import tpu as pltpu
from jax.experimental.pallas import tpu_sc as plsc

F32 = jnp.float32
BF16 = jnp.bfloat16

EPS = 1e-6
LB_FLOOR = 1e-30
NEG_BIG = -1e4
GRID_W = 64
ROPE_THETA = 10000.0
HD = 64
A_HEADS = 4
A_WIDTH = A_HEADS * HD
B_GROUPS = 4
POOL_WINDOWS = (2, 4, 8, 16)
POOL_HALO = 8
C_HEADS = 8
C_KV_HEADS = 2
C_GROUP = C_HEADS // C_KV_HEADS
N_EXPERTS = 16
N_GROUPS = 4
E_PER_GROUP = N_EXPERTS // N_GROUPS

TOKEN_TILE = 256
TOKEN_NSUB = 3
HGRN_CHUNK = 128
ATTN_TQ = 256
ATTN_UNROLL = 33
LOG2E = 1.4426950408889634
BOUND_SLACK = 1.02
BOUND_MAX = 48.0
MOE_BM = 256
MOE_NC = 4
MOE_CW = 256
SC_WINDOW = 128
VMEM_LIMIT = 56 * 1024 * 1024


def _cparams(sem):
    return pltpu.CompilerParams(dimension_semantics=sem, vmem_limit_bytes=VMEM_LIMIT)


def _dot(a, b):
    return jnp.dot(a, b, preferred_element_type=F32)


def _dot_nt(a, b):
    return lax.dot_general(a, b, (((1,), (1,)), ((), ())), preferred_element_type=F32)


def _dot_tn(a, b):
    return lax.dot_general(a, b, (((0,), (0,)), ((), ())), preferred_element_type=F32)


def _split_bf16(x):
    hi = x.astype(BF16)
    lo = (x - hi.astype(F32)).astype(BF16)
    return hi, lo


def _seg_mean_sq(x, bd):
    hi, lo = _split_bf16(x * x)
    return (_dot(hi, bd) + _dot(lo, bd)) * (1.0 / HD)


def _swap_halves(x):
    w = x.shape[-1]
    lane = lax.broadcasted_iota(jnp.int32, x.shape, x.ndim - 1)
    up = pltpu.roll(x, w - HD // 2, axis=x.ndim - 1)
    dn = pltpu.roll(x, HD // 2, axis=x.ndim - 1)
    return jnp.where((lane & (HD // 2)) == 0, up, dn)


def _mod_kernel(c_ref, w_ref, b_ref, o_ref):
    c = c_ref[...]
    a = c * jax.nn.sigmoid(c)
    hi, lo = _split_bf16(a)
    whi, wlo = _split_bf16(w_ref[0])
    o_ref[0] = _dot(hi, whi) + _dot(lo, whi) + _dot(hi, wlo) + b_ref[0]


def _mod_call(cond, w_mod, b_mod):
    depth, d, n6 = w_mod.shape
    tn = 1536
    return pl.pallas_call(
        _mod_kernel,
        out_shape=jax.ShapeDtypeStruct((depth, 8, n6), F32),
        grid=(depth, n6 // tn),
        in_specs=[
            pl.BlockSpec((8, d), lambda l, j: (0, 0)),
            pl.BlockSpec((1, d, tn), lambda l, j: (l, 0, j)),
            pl.BlockSpec((1, 1, tn), lambda l, j: (l, 0, j)),
        ],
        out_specs=pl.BlockSpec((1, 8, tn), lambda l, j: (l, 0, j)),
        compiler_params=_cparams(("arbitrary", "arbitrary")),
        name="adaln_mod",
    )(cond, w_mod, b_mod.reshape(depth, 1, n6))


def _moe_combine(y_ref, w_ref, rows):
    w0 = w_ref[0, rows, 0:1]
    w1 = w_ref[0, rows, 1:2]
    return jnp.concatenate([w0 * y_ref[c, 0, 0, rows, :] + w1 * y_ref[c, 0, 1, rows, :] for c in range(MOE_NC)],
                           axis=1)


def _in_kernel(*refs, has_moe, nsub, nctx):
    if has_moe:
        (x_ref, y_ref, wc_ref, modp_ref, modpc_ref, mod_ref, modc_ref, g_ref, w_ref, cos_ref, sin_ref, gq_ref,
         gk_ref, bdq_ref, bdk_ref, selq_ref, xo_ref, hg_ref, pool_ref, q_ref, k_ref, v_ref, qn2_ref,
         kn2_ref) = refs
    else:
        (x_ref, mod_ref, modc_ref, g_ref, w_ref, cos_ref, sin_ref, gq_ref, gk_ref,
         bdq_ref, bdk_ref, selq_ref, hg_ref, pool_ref, q_ref, k_ref, v_ref, qn2_ref, kn2_ref) = refs
    tm = TOKEN_TILE
    na = 5 * A_WIDTH
    q0 = na + 256
    qw = C_HEADS * HD
    kw = C_KV_HEADS * HD
    for r in range(nsub):
        rows = slice(r * tm, (r + 1) * tm)
        is_ctx = pl.program_id(1) * nsub + r < nctx
        mod = jnp.where(is_ctx, modc_ref[0, 0], mod_ref[0, 0])
        x = x_ref[0, rows, :]
        if has_moe:
            ga2 = jnp.where(is_ctx, modpc_ref[0, 0, 5:6, :], modp_ref[0, 0, 5:6, :])
            x = x + ga2 * _moe_combine(y_ref, wc_ref, rows)
            xo_ref[0, rows, :] = x
        ms = jnp.mean(x * x, axis=-1, keepdims=True)
        h = x * lax.rsqrt(ms + EPS) * g_ref[...]
        h = h * (1.0 + mod[1:2, :]) + mod[0:1, :]
        y = _dot(h.astype(BF16), w_ref[0])
        hg_ref[0, rows, :] = y[:, :na]
        pool_ref[0, rows, :] = y[:, na:na + 256]
        cos = cos_ref[rows, :]
        sin = sin_ref[rows, :]
        q = y[:, q0:q0 + qw]
        qn = q * lax.rsqrt(_seg_mean_sq(q, bdq_ref[...]) + EPS) * gq_ref[...]
        cos_q = jnp.concatenate([cos] * (qw // 128), axis=1)
        sin_q = jnp.concatenate([sin] * (qw // 128), axis=1)
        qr = (qn * cos_q + _swap_halves(qn) * sin_q) * (HD ** -0.5 * LOG2E)
        for hh in range(C_HEADS):
            q_ref[0, hh, rows, :] = qr[:, hh * HD:(hh + 1) * HD].astype(BF16)
        k = y[:, q0 + qw:q0 + qw + kw]
        kn = k * lax.rsqrt(_seg_mean_sq(k, bdk_ref[...]) + EPS) * gk_ref[...]
        kr = kn * cos + _swap_halves(kn) * sin
        v = y[:, q0 + qw + kw:q0 + qw + 2 * kw]
        vt = v.T.astype(BF16)
        for hh in range(C_KV_HEADS):
            k_ref[0, hh, rows, :] = kr[:, hh * HD:(hh + 1) * HD].astype(BF16)
            v_ref[0, hh, r] = vt[hh * HD:(hh + 1) * HD, :]
        q2hi, q2lo = _split_bf16(qr * qr)
        qn2 = _dot_nt(selq_ref[...], q2hi) + _dot_nt(selq_ref[...], q2lo)
        for hh in range(C_KV_HEADS):
            qn2_ref[0, hh, :, rows] = qn2[hh * C_GROUP:(hh + 1) * C_GROUP, :]
        k2hi, k2lo = _split_bf16(kr * kr)
        kn2 = _dot(k2hi, bdk_ref[...]) + _dot(k2lo, bdk_ref[...])
        kn2_ref[0, r] = jnp.max(kn2, axis=0, keepdims=True)


def _in_call(xs, moe, mods_prev, mods, g1, w_in, layer, cos_t, sin_t, gq, gk, bdq, bdk, ctx_len):
    bsz, s, d = xs.shape
    tm = TOKEN_TILE
    nsub = TOKEN_NSUB
    bt = nsub * tm
    ncols = w_in.shape[2]
    has_moe = moe is not None

    def tok(b, j):
        return (b, j, 0)

    batch_mod = pl.BlockSpec((1, 1, 8, d), lambda b, j: (0, b, 0, 0))
    ctx_mod = pl.BlockSpec((1, 1, 8, d), lambda b, j: (0, bsz, 0, 0))
    const2 = lambda b, j: (0, 0)
    in_specs = [pl.BlockSpec((1, bt, d), tok)]
    args = [xs]
    if has_moe:
        in_specs += [pl.BlockSpec((MOE_NC, 1, 2, bt, MOE_CW), lambda b, j: (0, b, 0, j, 0)),
                     pl.BlockSpec((1, bt, 128), tok), batch_mod, ctx_mod]
        args += [moe[0], moe[1], mods_prev, mods_prev]
    in_specs += [
        batch_mod,
        ctx_mod,
        pl.BlockSpec((1, d), const2),
        pl.BlockSpec((1, d, ncols), lambda b, j: (layer, 0, 0)),
        pl.BlockSpec((bt, 128), lambda b, j: (j, 0)),
        pl.BlockSpec((bt, 128), lambda b, j: (j, 0)),
        pl.BlockSpec((1, C_HEADS * HD), const2),
        pl.BlockSpec((1, C_KV_HEADS * HD), const2),
        pl.BlockSpec((C_HEADS * HD, C_HEADS * HD), const2),
        pl.BlockSpec((C_KV_HEADS * HD, C_KV_HEADS * HD), const2),
        pl.BlockSpec((C_HEADS, C_HEADS * HD), const2),
    ]
    selq = bdq[::HD]
    args += [mods, mods, g1, w_in, cos_t, sin_t, gq, gk, bdq, bdk, selq]
    out_shape = []
    out_specs = []
    if has_moe:
        out_shape.append(jax.ShapeDtypeStruct((bsz, s, d), F32))
        out_specs.append(pl.BlockSpec((1, bt, d), tok))
    out_shape += [
        jax.ShapeDtypeStruct((bsz, s, 5 * A_WIDTH), F32),
        jax.ShapeDtypeStruct((bsz, s, 256), F32),
        jax.ShapeDtypeStruct((bsz, C_HEADS, s, HD), BF16),
        jax.ShapeDtypeStruct((bsz, C_KV_HEADS, s, HD), BF16),
        jax.ShapeDtypeStruct((bsz, C_KV_HEADS, s // tm, HD, tm), BF16),
        jax.ShapeDtypeStruct((bsz, C_KV_HEADS, C_GROUP, s), F32),
        jax.ShapeDtypeStruct((bsz, s // tm, 1, C_KV_HEADS * HD), F32),
    ]
    out_specs += [
        pl.BlockSpec((1, bt, 5 * A_WIDTH), tok),
        pl.BlockSpec((1, bt, 256), tok),
        pl.BlockSpec((1, C_HEADS, bt, HD), lambda b, j: (b, 0, j, 0)),
        pl.BlockSpec((1, C_KV_HEADS, bt, HD), lambda b, j: (b, 0, j, 0)),
        pl.BlockSpec((1, C_KV_HEADS, nsub, HD, tm), lambda b, j: (b, 0, j, 0, 0)),
        pl.BlockSpec((1, C_KV_HEADS, C_GROUP, bt), lambda b, j: (b, 0, 0, j)),
        pl.BlockSpec((1, nsub, 1, C_KV_HEADS * HD), lambda b, j: (b, j, 0, 0)),
    ]
    return pl.pallas_call(
        functools.partial(_in_kernel, has_moe=has_moe, nsub=nsub, nctx=ctx_len // tm),
        out_shape=out_shape,
        grid=(bsz, s // bt),
        in_specs=in_specs,
        out_specs=out_specs,
        compiler_params=_cparams(("parallel", "arbitrary")),
        name="norm_in_proj",
    )(*args)


def _bcast_rows(x, period, off):
    n, w = x.shape
    if period >= 8:
        g = x.reshape(n // period, period, w)
        return jnp.broadcast_to(g[:, off:off + 1, :], g.shape).reshape(n, w)
    g = x.reshape(n // 8, 8, w)
    sub = lax.broadcasted_iota(jnp.int32, g.shape, 1)
    out = jnp.broadcast_to(g[:, off:off + 1, :], g.shape)
    for i in range(1, 8 // period):
        piece = jnp.broadcast_to(g[:, i * period + off:i * period + off + 1, :], g.shape)
        out = jnp.where(sub >= i * period, piece, out)
    return out.reshape(n, w)


def _hgrn_chunk(qz, fz, vz, lb, st_ref, rev, chunk):
    w = A_WIDTH
    row = lax.broadcasted_iota(jnp.int32, (chunk, w), 0)
    tau = (chunk - 1 - row) if rev else row

    q = qz * jax.nn.sigmoid(qz)
    lbf = jnp.maximum(lb, LB_FLOOR)
    e = jnp.exp(-jnp.abs(fz))
    r = 1.0 / (1.0 + e)
    er = e * r
    pos = fz >= 0
    f = lbf + (1.0 - lbf) * jnp.where(pos, r, er)
    kk = (1.0 - lbf) * jnp.where(pos, er, r)
    b = jnp.log(f)
    step = 1
    while step < chunk:
        b = b + jnp.where(tau >= step, pltpu.roll(b, (chunk - step) if rev else step, axis=0), 0.0)
        step *= 2

    ri = lax.broadcasted_iota(jnp.int32, (chunk, chunk), 0)
    ci = lax.broadcasted_iota(jnp.int32, (chunk, chunk), 1)
    if rev:
        ri = chunk - 1 - ri
        ci = chunk - 1 - ci

    def pair_products(qt, kt, mask):
        qt = qt.astype(BF16)
        kt = kt.astype(BF16)
        out = []
        for hh in range(A_HEADS):
            sl = slice(hh * HD, (hh + 1) * HD)
            prod = _dot_nt(qt[:, sl], kt[:, sl])
            out.append(prod if mask is None else jnp.where(mask, prod, 0.0))
        return out

    n_levels = int(math.log2(chunk))
    att = pair_products(q, kk, ri == ci)
    for level in range(n_levels):
        c = 1 << level
        upper = (tau & c) != 0
        if level == 0:
            qt = jnp.where(upper, q * f, 0.0)
            kt = jnp.where(upper, 0.0, kk)
        else:
            mid = _bcast_rows(b, 2 * c, c if rev else c - 1)
            z = jnp.where(upper, q, kk) * jnp.exp(-jnp.abs(b - mid))
            qt = jnp.where(upper, z, 0.0)
            kt = jnp.where(upper, 0.0, z)
        same_block = None if level == n_levels - 1 else (ri >> (level + 1)) == (ci >> (level + 1))
        att = [a + p for a, p in zip(att, pair_products(qt, kt, same_block))]

    b_end = b[0:1, :] if rev else b[chunk - 1:chunk, :]
    qin = (q * jnp.exp(b)).astype(BF16)
    kend = (kk * jnp.exp(b_end - b)).astype(BF16)
    vb = vz.astype(BF16)
    decay = jnp.exp(b_end)
    outs = []
    for hh in range(A_HEADS):
        sl = slice(hh * HD, (hh + 1) * HD)
        st = st_ref[hh]
        o = _dot_nt(qin[:, sl], st.astype(BF16)) + _dot(att[hh].astype(BF16), vb[:, sl])
        st_ref[hh] = st * decay[:, sl] + _dot_tn(vb[:, sl], kend[:, sl])
        outs.append(o)
    return jnp.concatenate(outs, axis=1)


def _hgrn_kernel(qf_ref, ff_ref, vf_ref, qb_ref, fb_ref, vb_ref, lb_ref, of_ref, ob_ref, st_ref, *, chunk):
    @pl.when(pl.program_id(1) == 0)
    def _():
        st_ref[...] = jnp.zeros_like(st_ref)

    of_ref[0] = _hgrn_chunk(qf_ref[0], ff_ref[0], vf_ref[0], lb_ref[0:1, :], st_ref.at[0], False, chunk)
    ob_ref[0] = _hgrn_chunk(qb_ref[0], fb_ref[0], vb_ref[0], lb_ref[1:2, :], st_ref.at[1], True, chunk)


def _hgrn_call(hg, lower, ctx_len):
    bsz, s, _ = hg.shape
    chunk = HGRN_CHUNK
    n = s // chunk
    nctx = ctx_len // chunk
    w = A_WIDTH

    def rmap(i):
        return jnp.where(i < nctx, nctx - 1 - i, n + nctx - 1 - i)

    def fwd(col):
        return pl.BlockSpec((1, chunk, w), lambda b, i: (b, i, col))

    def bwd(col):
        return pl.BlockSpec((1, chunk, w), lambda b, i: (b, rmap(i), col))

    return pl.pallas_call(
        functools.partial(_hgrn_kernel, chunk=chunk),
        out_shape=[jax.ShapeDtypeStruct((bsz, s, w), F32)] * 2,
        grid=(bsz, n),
        in_specs=[fwd(0), fwd(1), fwd(3), bwd(0), bwd(2), bwd(3),
                  pl.BlockSpec((2, w), lambda b, i: (0, 0))],
        out_specs=[pl.BlockSpec((1, chunk, w), lambda b, i: (b, i, 0)),
                   pl.BlockSpec((1, chunk, w), lambda b, i: (b, rmap(i), 0))],
        scratch_shapes=[pltpu.VMEM((2, A_HEADS, HD, HD), F32)],
        compiler_params=_cparams(("parallel", "arbitrary")),
        name="hgrn2_scan",
    )(hg, hg, hg, hg, hg, hg, lower)


def _attn_kernel(kmax_ref, q_ref, k_ref, vt_ref, qn2_ref, o_ref, *, tq, ts, n_sub, n_ctx_sub, nctx_q, unroll,
                 bounded):
    qi = pl.program_id(2)
    w = C_GROUP * tq
    q = q_ref[0].reshape(w, HD)

    def keys(c0, nc):
        return k_ref[0, 0, pl.ds(pl.multiple_of(c0 * ts, ts), nc * ts), :]

    def values_dot(c0, nc, p):
        pb = p.astype(BF16)
        out = _dot(vt_ref[0, 0, c0], pb[0:ts])
        for c in range(1, nc):
            out = out + _dot(vt_ref[0, 0, c0 + c], pb[c * ts:(c + 1) * ts])
        return out

    if bounded:
        kmax = kmax_ref[pl.program_id(0) * C_KV_HEADS + pl.program_id(1)]
        qn2 = jnp.concatenate([qn2_ref[0, 0, g:g + 1, :] for g in range(C_GROUP)], axis=1)
        shift = jnp.sqrt(qn2) * kmax

        def sub_tile(c0, nc, carry):
            l8, acc = carry
            p = jnp.exp2(_dot_nt(keys(c0, nc), q) - shift)
            l8 = l8 + jnp.sum(p.reshape(nc * ts // 8, 8, w), axis=0)
            return l8, acc + values_dot(c0, nc, p)

        init = (jnp.zeros((8, w), F32), jnp.zeros((HD, w), F32))
    else:
        def sub_tile(c0, nc, carry):
            m_prev, l_prev, acc = carry
            s = _dot_nt(keys(c0, nc), q)
            m_new = jnp.maximum(m_prev, jnp.max(s, axis=0, keepdims=True))
            alpha = jnp.exp2(m_prev - m_new)
            p = jnp.exp2(s - m_new)
            l_new = alpha * l_prev + jnp.sum(p, axis=0, keepdims=True)
            return m_new, l_new, alpha * acc + values_dot(c0, nc, p)

        init = (jnp.full((1, w), -jnp.inf, F32), jnp.zeros((1, w), F32), jnp.zeros((HD, w), F32))

    def run(n):
        carry = init
        if n >= unroll:
            carry = lax.fori_loop(0, n // unroll, lambda i, c: sub_tile(i * unroll, unroll, c), carry)
        if n % unroll:
            carry = sub_tile((n // unroll) * unroll, n % unroll, carry)
        l_fin = jnp.sum(carry[-2], axis=0, keepdims=True)
        ot = carry[-1] / l_fin
        stacked = jnp.concatenate([ot[:, g * tq:(g + 1) * tq] for g in range(C_GROUP)], axis=0)
        o_ref[0] = stacked.T.astype(o_ref.dtype)

    @pl.when(qi < nctx_q)
    def _():
        run(n_ctx_sub)

    @pl.when(qi >= nctx_q)
    def _():
        run(n_sub)


def _attn_call(kmax, q, k, vt, qn2, ctx_len, bounded):
    bsz, _, s, _ = q.shape
    tq = ATTN_TQ
    ts = vt.shape[-1]
    assert ctx_len % ts == 0
    n_sub = s // ts
    grid_spec = pltpu.PrefetchScalarGridSpec(
        num_scalar_prefetch=1,
        grid=(bsz, C_KV_HEADS, s // tq),
        in_specs=[
            pl.BlockSpec((1, C_GROUP, tq, HD), lambda b, g, qi, km: (b, g, qi, 0)),
            pl.BlockSpec((1, 1, s, HD), lambda b, g, qi, km: (b, g, 0, 0)),
            pl.BlockSpec((1, 1, n_sub, HD, ts), lambda b, g, qi, km: (b, g, 0, 0, 0)),
            pl.BlockSpec((1, 1, C_GROUP, tq), lambda b, g, qi, km: (b, g, 0, qi)),
        ],
        out_specs=pl.BlockSpec((1, tq, C_GROUP * HD), lambda b, g, qi, km: (b, qi, g)),
    )
    return pl.pallas_call(
        functools.partial(_attn_kernel, tq=tq, ts=ts, n_sub=n_sub, n_ctx_sub=ctx_len // ts,
                          nctx_q=ctx_len // tq, unroll=ATTN_UNROLL, bounded=bounded),
        out_shape=jax.ShapeDtypeStruct((bsz, s, C_HEADS * HD), BF16),
        grid_spec=grid_spec,
        compiler_params=_cparams(("parallel", "parallel", "arbitrary")),
        name="gqa_attention_bounded" if bounded else "gqa_attention",
    )(kmax, q, k, vt, qn2)


def _attention(q, k, vt, qn2, kn2, ctx_len):
    bsz = q.shape[0]
    k2 = jnp.max(kn2, axis=(1, 2)).reshape(bsz, C_KV_HEADS, HD)[:, :, 0]
    q2 = jnp.max(qn2.reshape(bsz, C_KV_HEADS, -1), axis=-1)
    kmax = (jnp.sqrt(k2) * BOUND_SLACK).reshape(-1)
    small = jnp.max(jnp.sqrt(q2).reshape(-1) * kmax) <= BOUND_MAX
    return lax.cond(small,
                    functools.partial(_attn_call, ctx_len=ctx_len, bounded=True),
                    functools.partial(_attn_call, ctx_len=ctx_len, bounded=False),
                    kmax, q, k, vt, qn2)


def _route(logits_t, bias):
    scores = jax.nn.sigmoid(logits_t)
    sel = scores + bias
    rows = [sel[e:e + 1, :] for e in range(N_EXPERTS)]
    srow = [scores[e:e + 1, :] for e in range(N_EXPERTS)]
    best_val = None
    best_grp = None
    for g in range(N_GROUPS):
        a = rows[g * E_PER_GROUP:(g + 1) * E_PER_GROUP]
        m1 = a[0]
        i1 = jnp.zeros_like(a[0], dtype=jnp.int32)
        for j in range(1, E_PER_GROUP):
            take = a[j] > m1
            m1 = jnp.where(take, a[j], m1)
            i1 = jnp.where(take, j, i1)
        m2 = jnp.full_like(m1, -jnp.inf)
        for j in range(E_PER_GROUP):
            m2 = jnp.where(i1 == j, m2, jnp.maximum(m2, a[j]))
        gs = m1 + m2
        if g == 0:
            best_val, best_grp = gs, jnp.zeros_like(i1)
        else:
            take = gs > best_val
            best_val = jnp.where(take, gs, best_val)
            best_grp = jnp.where(take, g, best_grp)
    masked = [jnp.where(best_grp == (e // E_PER_GROUP), rows[e], NEG_BIG) for e in range(N_EXPERTS)]
    v1 = masked[0]
    e1 = jnp.zeros_like(best_grp)
    for e in range(1, N_EXPERTS):
        take = masked[e] > v1
        v1 = jnp.where(take, masked[e], v1)
        e1 = jnp.where(take, e, e1)
    v2 = jnp.full_like(v1, -jnp.inf)
    e2 = jnp.zeros_like(best_grp)
    for e in range(N_EXPERTS):
        take = jnp.logical_and(e1 != e, masked[e] > v2)
        v2 = jnp.where(take, masked[e], v2)
        e2 = jnp.where(take, e, e2)
    w1 = jnp.zeros_like(v1)
    w2 = jnp.zeros_like(v1)
    for e in range(N_EXPERTS):
        w1 = jnp.where(e1 == e, srow[e], w1)
        w2 = jnp.where(e2 == e, srow[e], w2)
    tot = w1 + w2
    return jnp.concatenate([e1, e2], axis=0), jnp.concatenate([w1 / tot, w2 / tot], axis=0)


def _out_kernel(x_ref, of_ref, ob_ref, g_ref, pc_ref, pp_ref, pn_ref, oc_ref, mod_ref, modc_ref, ng_ref, pw_ref,
                ps_ref, wo_ref, g2_ref, rw_ref, rb_ref, bda_ref,
                xo_ref, h2_ref, idx_ref, wgt_ref, *, nsub, seg_tiles):
    tm = TOKEN_TILE
    nctx, ntot = seg_tiles
    n = tm + 2 * POOL_HALO
    lane = lax.broadcasted_iota(jnp.int32, (tm, 256), 1)
    rhi, rlo = _split_bf16(rw_ref[...])
    for r in range(nsub):
        rows = slice(r * tm, (r + 1) * tm)
        j = pl.program_id(1) * nsub + r
        mod = jnp.where(j < nctx, modc_ref[0, 0], mod_ref[0, 0])
        o = of_ref[0, rows, :] + ob_ref[0, rows, :]
        gz = g_ref[0, rows, :]
        oa = o * lax.rsqrt(_seg_mean_sq(o, bda_ref[...]) + EPS) * ng_ref[...] * (gz * jax.nn.sigmoid(gz))
        first = jnp.logical_or(j == 0, j == nctx)
        last = jnp.logical_or(j == nctx - 1, j == ntot - 1)
        cur = pc_ref[0, rows, :]
        before = pp_ref[0] if r == 0 else pc_ref[0, r * tm - POOL_HALO:r * tm, :]
        after = pn_ref[0] if r == nsub - 1 else pc_ref[0, (r + 1) * tm:(r + 1) * tm + POOL_HALO, :]
        ext = jnp.concatenate([jnp.where(first, 0.0, before), cur, jnp.where(last, 0.0, after)], axis=0)
        s2 = ext + pltpu.roll(ext, 1, axis=0)
        s4 = pltpu.roll(s2, 1, axis=0) + pltpu.roll(s2, n - 1, axis=0)
        s8 = pltpu.roll(s4, 2, axis=0) + pltpu.roll(s4, n - 2, axis=0)
        s16 = pltpu.roll(s8, 4, axis=0) + pltpu.roll(s8, n - 4, axis=0)
        sums = [a[POOL_HALO:POOL_HALO + tm, :] for a in (s2, s4, s8, s16)]
        seg_start = jnp.where(j < nctx, 0, nctx) * tm
        seg_len = jnp.where(j < nctx, nctx, ntot - nctx) * tm
        t = j * tm - seg_start + lax.broadcasted_iota(jnp.int32, (tm, 1), 0)
        mean = jnp.zeros((tm, 256), F32)
        for gi, win in enumerate(POOL_WINDOWS):
            cnt = (jnp.minimum(t + win // 2, seg_len) - jnp.maximum(t - win // 2, 0)).astype(F32)
            mean = jnp.where((lane >> 6) == gi, sums[gi] / cnt, mean)
        ob = _dot((mean - cur).astype(BF16), pw_ref[...]) * ps_ref[...]
        mix = (_dot(oa.astype(BF16), wo_ref[0, 0:256, :]) + _dot(ob.astype(BF16), wo_ref[0, 256:512, :])
               + _dot(oc_ref[0, rows, :], wo_ref[0, 512:1024, :]))
        x = x_ref[0, rows, :] + mod[2:3, :] * mix
        xo_ref[0, rows, :] = x
        ms = jnp.mean(x * x, axis=-1, keepdims=True)
        h2 = x * lax.rsqrt(ms + EPS) * g2_ref[...]
        h2 = h2 * (1.0 + mod[4:5, :]) + mod[3:4, :]
        for c in range(MOE_NC):
            h2_ref[c, rows, :] = h2[:, c * MOE_CW:(c + 1) * MOE_CW]
        hhi, hlo = _split_bf16(h2)
        logits_t = _dot_nt(rhi, hhi) + _dot_nt(rhi, hlo) + _dot_nt(rlo, hhi)
        ids, wts = _route(logits_t, rb_ref[...])
        idx_ref[0, :, rows] = ids
        wgt_ref[0, rows, :] = jnp.concatenate([wts, jnp.zeros((128 - 2, tm), F32)], axis=0).T


def _out_call(xs, of, ob, hg, pool, oc, mods, ng, pw, ps, wo, layer, g2, rw_t, rb, bda, ctx_len):
    bsz, s, d = xs.shape
    tm = TOKEN_TILE
    nsub = TOKEN_NSUB
    bt = nsub * tm
    nt = s // bt
    hb = bt // POOL_HALO
    nhb = s // POOL_HALO

    def tok(b, j):
        return (b, j, 0)

    const2 = lambda b, j: (0, 0)
    return pl.pallas_call(
        functools.partial(_out_kernel, nsub=nsub, seg_tiles=(ctx_len // tm, s // tm)),
        out_shape=[
            jax.ShapeDtypeStruct((bsz, s, d), F32),
            jax.ShapeDtypeStruct((MOE_NC, bsz * s, MOE_CW), F32),
            jax.ShapeDtypeStruct((bsz, 2, s), jnp.int32),
            jax.ShapeDtypeStruct((bsz, s, 128), F32),
        ],
        grid=(bsz, nt),
        in_specs=[
            pl.BlockSpec((1, bt, d), tok),
            pl.BlockSpec((1, bt, A_WIDTH), tok),
            pl.BlockSpec((1, bt, A_WIDTH), tok),
            pl.BlockSpec((1, bt, A_WIDTH), lambda b, j: (b, j, 4)),
            pl.BlockSpec((1, bt, 256), tok),
            pl.BlockSpec((1, POOL_HALO, 256), lambda b, j: (b, jnp.maximum(j * hb - 1, 0), 0)),
            pl.BlockSpec((1, POOL_HALO, 256), lambda b, j: (b, jnp.minimum((j + 1) * hb, nhb - 1), 0)),
            pl.BlockSpec((1, bt, C_HEADS * HD), tok),
            pl.BlockSpec((1, 1, 8, d), lambda b, j: (0, b, 0, 0)),
            pl.BlockSpec((1, 1, 8, d), lambda b, j: (0, bsz, 0, 0)),
            pl.BlockSpec((1, A_WIDTH), const2),
            pl.BlockSpec((256, 256), const2),
            pl.BlockSpec((1, 256), const2),
            pl.BlockSpec((1, d, d), lambda b, j: (layer, 0, 0)),
            pl.BlockSpec((1, d), const2),
            pl.BlockSpec((N_EXPERTS, d), const2),
            pl.BlockSpec((N_EXPERTS, 1), const2),
            pl.BlockSpec((A_WIDTH, A_WIDTH), const2),
        ],
        out_specs=[
            pl.BlockSpec((1, bt, d), tok),
            pl.BlockSpec((MOE_NC, bt, MOE_CW), lambda b, j: (0, b * nt + j, 0)),
            pl.BlockSpec((1, 2, bt), lambda b, j: (b, 0, j)),
            pl.BlockSpec((1, bt, 128), tok),
        ],
        compiler_params=_cparams(("parallel", "arbitrary")),
        name="mix_out_proj_router",
    )(xs, of, ob, hg, pool, pool, pool, oc, mods, mods, ng, pw, ps, wo, g2, rw_t, rb, bda)


def _moe_kernel(be_ref, nb_ref, x_ref, wg_ref, wu_ref, wd_ref, y_ref, wgb, wub, wdb):
    i = pl.program_id(0)
    prev = be_ref[jnp.maximum(i - 1, 0)]
    fresh = jnp.logical_or(i == 0, be_ref[i] != prev)

    @pl.when(fresh)
    def _():
        wgb[...] = wg_ref[0, 0].astype(BF16)
        wub[...] = wu_ref[0, 0].astype(BF16)
        wdb[...] = wd_ref[0, 0].astype(BF16)

    @pl.when(i < nb_ref[0])
    def _():
        x = jnp.concatenate([x_ref[c] for c in range(MOE_NC)], axis=1).astype(BF16)
        gate = _dot(x, wgb[...])
        up = _dot(x, wub[...])
        a = gate * jax.nn.sigmoid(gate) * up
        y = _dot(a.astype(BF16), wdb[...])
        for c in range(MOE_NC):
            y_ref[c] = y[:, c * MOE_CW:(c + 1) * MOE_CW]

    @pl.when(i >= nb_ref[0])
    def _():
        y_ref[...] = jnp.zeros_like(y_ref)


def _moe_call(block_expert, nblocks, xg, wg, wu, wd, layer):
    nc, p, cw = xg.shape
    _, _, d, f = wg.shape
    bm = MOE_BM
    grid_spec = pltpu.PrefetchScalarGridSpec(
        num_scalar_prefetch=2,
        grid=(p // bm,),
        in_specs=[
            pl.BlockSpec((nc, bm, cw), lambda i, be, nb: (0, i, 0)),
            pl.BlockSpec((1, 1, d, f), lambda i, be, nb: (layer, be[i], 0, 0)),
            pl.BlockSpec((1, 1, d, f), lambda i, be, nb: (layer, be[i], 0, 0)),
            pl.BlockSpec((1, 1, f, d), lambda i, be, nb: (layer, be[i], 0, 0)),
        ],
        out_specs=pl.BlockSpec((nc, bm, cw), lambda i, be, nb: (0, i, 0)),
        scratch_shapes=[pltpu.VMEM((d, f), BF16), pltpu.VMEM((d, f), BF16), pltpu.VMEM((f, d), BF16)],
    )
    return pl.pallas_call(
        _moe_kernel,
        out_shape=jax.ShapeDtypeStruct((nc, p, cw), F32),
        grid_spec=grid_spec,
        compiler_params=_cparams(("arbitrary",)),
        name="moe_experts",
    )(block_expert, nblocks, xg, wg, wu, wd)


def _sc_mesh():
    return plsc.VectorSubcoreMesh(core_axis_name="core", subcore_axis_name="subcore")


def _sc_scatter_rows(x, idx, n_out, src_block):
    m = idx.shape[0]

    @pl.kernel(out_type=jax.ShapeDtypeStruct((n_out, x.shape[1]), x.dtype), mesh=_sc_mesh())
    def scatter(x_hbm, i_hbm, o_hbm):
        def body(x_vmem, i_vmem):
            pltpu.sync_copy(x_vmem, o_hbm.at[i_vmem.at[0]])

        pltpu.emit_pipeline(
            body,
            grid=(m // SC_WINDOW,),
            in_specs=[pl.BlockSpec((SC_WINDOW, x.shape[1]), lambda w: (src_block(w), 0)),
                      pl.BlockSpec((1, SC_WINDOW), lambda w: (0, w))],
            out_specs=[],
            core_axis_name=("core", "subcore"),
            dimension_semantics=(pltpu.PARALLEL,),
        )(x_hbm, i_hbm)

    return scatter(x, idx.reshape(1, m))


def _sc_gather_rows(x, idx):
    m = idx.shape[0]

    @pl.kernel(out_type=jax.ShapeDtypeStruct((m, x.shape[1]), x.dtype), mesh=_sc_mesh())
    def gather(x_hbm, i_hbm, o_hbm):
        def body(i_vmem, o_vmem):
            pltpu.sync_copy(x_hbm.at[i_vmem.at[0]], o_vmem)

        pltpu.emit_pipeline(
            body,
            grid=(m // SC_WINDOW,),
            in_specs=[pl.BlockSpec((1, SC_WINDOW), lambda w: (0, w))],
            out_specs=[pl.BlockSpec((SC_WINDOW, x.shape[1]), lambda w: (w, 0))],
            core_axis_name=("core", "subcore"),
            dimension_semantics=(pltpu.PARALLEL,),
        )(i_hbm, o_hbm)

    return gather(x, idx.reshape(1, m))


def _moe_plan(idx, n_tokens):
    bm = MOE_BM
    flat = idx.reshape(-1)
    onehot = (flat[:, None] == jnp.arange(N_EXPERTS)[None, :]).astype(jnp.int32)
    rank = jnp.cumsum(onehot, axis=0) - onehot
    counts = jnp.sum(onehot, axis=0)
    padded = ((counts + bm - 1) // bm) * bm
    seg_end = jnp.cumsum(padded)
    seg_start = seg_end - padded
    slot = jnp.sum(onehot * (seg_start[None, :] + rank), axis=1)
    n_assign = flat.shape[0]
    p = (n_assign // bm + N_EXPERTS) * bm
    nb = p // bm
    blk_start = jnp.arange(nb) * bm
    block_expert = jnp.sum((blk_start[:, None] >= seg_end[None, :]).astype(jnp.int32), axis=1)
    nblocks = (seg_end[-1] // bm).astype(jnp.int32)
    last_e = jnp.max(jnp.where(counts > 0, jnp.arange(N_EXPERTS), 0))
    block_expert = jnp.minimum(block_expert, last_e).astype(jnp.int32)
    return slot, block_expert, nblocks.reshape(1), p


def _final_kernel(x_ref, y_ref, w_ref, mod_ref, o_ref):
    o_ref[0] = x_ref[0] + mod_ref[0, 0, 5:6, :] * _moe_combine(y_ref, w_ref, slice(None))


def _final_call(xs, yc, wcol, mods, ctx_len):
    bsz, s, d = xs.shape
    tm = TOKEN_TILE
    nctx = ctx_len // tm
    t = s - ctx_len
    return pl.pallas_call(
        _final_kernel,
        out_shape=jax.ShapeDtypeStruct((bsz, t, d), F32),
        grid=(bsz, t // tm),
        in_specs=[
            pl.BlockSpec((1, tm, d), lambda b, j: (b, j + nctx, 0)),
            pl.BlockSpec((MOE_NC, 1, 2, tm, MOE_CW), lambda b, j: (0, b, 0, j + nctx, 0)),
            pl.BlockSpec((1, tm, 128), lambda b, j: (b, j + nctx, 0)),
            pl.BlockSpec((1, 1, 8, d), lambda b, j: (0, b, 0, 0)),
        ],
        out_specs=pl.BlockSpec((1, tm, d), lambda b, j: (b, j, 0)),
        compiler_params=_cparams(("parallel", "arbitrary")),
        name="final_residual",
    )(xs, yc, wcol, mods)


def _block_diag_ones(width):
    i = np.arange(width) // HD
    return jnp.asarray((i[:, None] == i[None, :]).astype(np.float32), dtype=BF16)


def _rope_tables(t, ctx_len):
    rows_n = t // GRID_W
    row = jnp.repeat(jnp.arange(rows_n), GRID_W).astype(F32)
    col = jnp.tile(jnp.arange(GRID_W), rows_n).astype(F32)
    n_freq = HD // 4
    inv = ROPE_THETA ** (-jnp.arange(n_freq, dtype=F32) / n_freq)
    ang = jnp.concatenate([row[:, None] * inv, col[:, None] * inv], axis=-1)
    cos, sin = jnp.cos(ang), jnp.sin(ang)
    cos_h = jnp.concatenate([cos, cos], axis=-1)
    sin_h = jnp.concatenate([-sin, sin], axis=-1)
    cos_h = jnp.concatenate([jnp.ones((ctx_len, HD), F32), cos_h], axis=0)
    sin_h = jnp.concatenate([jnp.zeros((ctx_len, HD), F32), sin_h], axis=0)
    return jnp.tile(cos_h, (1, 2)), jnp.tile(sin_h, (1, 2))


def kernel(x, c, ctx, c_ctx, w_mod, b_mod, norm1_g, norm2_g, w_in, w_out, hgrn_lb, hgrn_norm_g,
           pool_w, pool_scale, q_norm_g, k_norm_g, router_w, router_b, moe_w_gate, moe_w_up, moe_w_down):
    bsz, t, d = x.shape
    ctx_len = ctx.shape[1]
    depth = w_mod.shape[0]
    s = ctx_len + t
    assert ctx_len % TOKEN_TILE == 0 and s % (TOKEN_TILE * TOKEN_NSUB) == 0 and t % GRID_W == 0
    assert ctx_len % HGRN_CHUNK == 0 and ctx_len % ATTN_TQ == 0

    perm = np.concatenate([np.arange(0, HD, 2), np.arange(1, HD, 2)])
    na = 5 * A_WIDTH
    q0 = na + 256
    nqk = (C_HEADS + C_KV_HEADS) * HD
    w_qk = w_in[:, :, q0:q0 + nqk].reshape(depth, d, nqk // HD, HD // 2, 2)
    w_qk = jnp.swapaxes(w_qk, -1, -2).reshape(depth, d, nqk)
    w_in_b = jnp.concatenate([w_in[:, :, :q0], w_qk, w_in[:, :, q0 + nqk:]], axis=-1).astype(BF16)
    w_out_b = w_out.astype(BF16)
    gq = jnp.tile(q_norm_g[:, perm], (1, C_HEADS)).reshape(depth, 1, C_HEADS * HD)
    gk = jnp.tile(k_norm_g[:, perm], (1, C_KV_HEADS)).reshape(depth, 1, C_KV_HEADS * HD)
    ng = jnp.tile(hgrn_norm_g, (1, A_HEADS)).reshape(depth, 1, A_WIDTH)
    eye = jnp.eye(B_GROUPS, dtype=F32)
    pw_bd = (eye[None, :, None, :, None] * pool_w[:, :, :, None, :]).reshape(depth, 256, 256).astype(BF16)
    sm = jax.nn.softmax(hgrn_lb.astype(F32), axis=0)
    lower = jnp.cumsum(sm, axis=0) - sm[0:1]
    cos_t, sin_t = _rope_tables(t, ctx_len)
    bdq, bdk, bda = _block_diag_ones(C_HEADS * HD), _block_diag_ones(C_KV_HEADS * HD), _block_diag_ones(A_WIDTH)
    rw_t = router_w.T
    rb = router_b.reshape(N_EXPERTS, 1).astype(F32)

    cond = jnp.zeros((8, d), F32).at[:bsz].set(c).at[bsz].set(c_ctx)
    mods = _mod_call(cond, w_mod, b_mod)
    mods = mods[:, :bsz + 1].reshape(depth, bsz + 1, 6, d)
    mods = jnp.pad(mods, ((0, 0), (0, 0), (0, 2), (0, 0)))

    xs = jnp.concatenate([ctx, x], axis=1)
    moe = None
    n = bsz * s
    wps = s // SC_WINDOW

    def src_block(w):
        j = w % wps
        bk = w // wps
        return (bk // (2 * bsz)) * (n // SC_WINDOW) + ((bk // 2) % bsz) * wps + j

    for l in range(depth):
        outs = _in_call(xs, moe, mods[l - 1:l] if l else None, mods[l:l + 1], norm1_g[l:l + 1], w_in_b, l,
                        cos_t, sin_t, gq[l], gk[l], bdq, bdk, ctx_len)
        if moe is not None:
            xs, outs = outs[0], outs[1:]
        hg, pool, qh, kh, vh, qn2, kn2 = outs
        of, ob = _hgrn_call(hg, lower[l], ctx_len)
        oc = _attention(qh, kh, vh, qn2, kn2, ctx_len)
        xs, h2c, idx, wcol = _out_call(xs, of, ob, hg, pool, oc, mods[l:l + 1], ng[l], pw_bd[l],
                                       pool_scale[l:l + 1], w_out_b, l, norm2_g[l:l + 1], rw_t, rb, bda, ctx_len)
        slot, block_expert, nblocks, p = _moe_plan(idx, n)
        rows = (jnp.arange(MOE_NC, dtype=jnp.int32)[:, None] * p + slot[None, :]).reshape(-1)
        xg = _sc_scatter_rows(h2c.reshape(MOE_NC * n, MOE_CW), rows, MOE_NC * p, src_block)
        ys = _moe_call(block_expert, nblocks, xg.reshape(MOE_NC, p, MOE_CW),
                       moe_w_gate, moe_w_up, moe_w_down, l)
        yc = _sc_gather_rows(ys.reshape(MOE_NC * p, MOE_CW), rows)
        moe = (yc.reshape(MOE_NC, bsz, 2, s, MOE_CW), wcol)
    return _final_call(xs, moe[0], moe[1], mods[depth - 1:depth], ctx_len)
```

```python
import functools
import math

import jax
import jax.numpy as jnp
import numpy as np
from jax import lax
from jax.experimental import pallas as pl
from jax.experimental.pallas import tpu as pltpu
from jax.experimental.pallas import tpu_sc as plsc

F32 = jnp.float32
BF16 = jnp.bfloat16

EPS = 1e-6
LB_FLOOR = 1e-30
NEG_BIG = -1e4
GRID_W = 64
ROPE_THETA = 10000.0
HD = 64
A_HEADS = 4
A_WIDTH = A_HEADS * HD
B_GROUPS = 4
POOL_WINDOWS = (2, 4, 8, 16)
POOL_HALO = 8
C_HEADS = 8
C_KV_HEADS = 2
C_GROUP = C_HEADS // C_KV_HEADS
N_EXPERTS = 16
N_GROUPS = 4
E_PER_GROUP = N_EXPERTS // N_GROUPS

TOKEN_TILE = 256
TOKEN_NSUB = 3
HGRN_CHUNK = 128
ATTN_TQ = 256
ATTN_UNROLL = 33
LOG2E = 1.4426950408889634
BOUND_SLACK = 1.02
BOUND_MAX = 48.0
MOE_BM = 512
MOE_NC = 4
MOE_CW = 256
SC_WINDOW = 128
VMEM_LIMIT = 56 * 1024 * 1024


def _cparams(sem):
    return pltpu.CompilerParams(dimension_semantics=sem, vmem_limit_bytes=VMEM_LIMIT)


def _dot(a, b):
    return jnp.dot(a, b, preferred_element_type=F32)


def _dot_nt(a, b):
    return lax.dot_general(a, b, (((1,), (1,)), ((), ())), preferred_element_type=F32)


def _dot_tn(a, b):
    return lax.dot_general(a, b, (((0,), (0,)), ((), ())), preferred_element_type=F32)


def _split_bf16(x):
    hi = x.astype(BF16)
    lo = (x - hi.astype(F32)).astype(BF16)
    return hi, lo


def _seg_mean_sq(x, bd):
    hi, lo = _split_bf16(x * x)
    return (_dot(hi, bd) + _dot(lo, bd)) * (1.0 / HD)


def _swap_halves(x):
    w = x.shape[-1]
    lane = lax.broadcasted_iota(jnp.int32, x.shape, x.ndim - 1)
    up = pltpu.roll(x, w - HD // 2, axis=x.ndim - 1)
    dn = pltpu.roll(x, HD // 2, axis=x.ndim - 1)
    return jnp.where((lane & (HD // 2)) == 0, up, dn)


def _mod_kernel(c_ref, w_ref, b_ref, o_ref):
    c = c_ref[...]
    a = c * jax.nn.sigmoid(c)
    hi, lo = _split_bf16(a)
    whi, wlo = _split_bf16(w_ref[0])
    o_ref[0] = _dot(hi, whi) + _dot(lo, whi) + _dot(hi, wlo) + b_ref[0]


def _mod_call(cond, w_mod, b_mod):
    depth, d, n6 = w_mod.shape
    tn = 1536
    return pl.pallas_call(
        _mod_kernel,
        out_shape=jax.ShapeDtypeStruct((depth, 8, n6), F32),
        grid=(depth, n6 // tn),
        in_specs=[
            pl.BlockSpec((8, d), lambda l, j: (0, 0)),
            pl.BlockSpec((1, d, tn), lambda l, j: (l, 0, j)),
            pl.BlockSpec((1, 1, tn), lambda l, j: (l, 0, j)),
        ],
        out_specs=pl.BlockSpec((1, 8, tn), lambda l, j: (l, 0, j)),
        compiler_params=_cparams(("arbitrary", "arbitrary")),
        name="adaln_mod",
    )(cond, w_mod, b_mod.reshape(depth, 1, n6))


def _moe_combine(y_ref, w_ref, rows):
    w0 = w_ref[0, rows, 0:1]
    w1 = w_ref[0, rows, 1:2]
    return jnp.concatenate([w0 * y_ref[c, 0, 0, rows, :] + w1 * y_ref[c, 0, 1, rows, :] for c in range(MOE_NC)],
                           axis=1)


def _in_kernel(*refs, has_moe, nsub, nctx):
    if has_moe:
        (x_ref, y_ref, wc_ref, modp_ref, modpc_ref, mod_ref, modc_ref, g_ref, w_ref, cos_ref, sin_ref, gq_ref,
         gk_ref, bdq_ref, bdk_ref, selq_ref, xo_ref, hg_ref, pool_ref, q_ref, k_ref, v_ref, qn2_ref,
         kn2_ref) = refs
    else:
        (x_ref, mod_ref, modc_ref, g_ref, w_ref, cos_ref, sin_ref, gq_ref, gk_ref,
         bdq_ref, bdk_ref, selq_ref, hg_ref, pool_ref, q_ref, k_ref, v_ref, qn2_ref, kn2_ref) = refs
    tm = TOKEN_TILE
    na = 5 * A_WIDTH
    q0 = na + 256
    qw = C_HEADS * HD
    kw = C_KV_HEADS * HD
    for r in range(nsub):
        rows = slice(r * tm, (r + 1) * tm)
        is_ctx = pl.program_id(1) * nsub + r < nctx
        mod = jnp.where(is_ctx, modc_ref[0, 0], mod_ref[0, 0])
        x = x_ref[0, rows, :]
        if has_moe:
            ga2 = jnp.where(is_ctx, modpc_ref[0, 0, 5:6, :], modp_ref[0, 0, 5:6, :])
            x = x + ga2 * _moe_combine(y_ref, wc_ref, rows)
            xo_ref[0, rows, :] = x
        ms = jnp.mean(x * x, axis=-1, keepdims=True)
        h = x * lax.rsqrt(ms + EPS) * g_ref[...]
        h = h * (1.0 + mod[1:2, :]) + mod[0:1, :]
        y = _dot(h.astype(BF16), w_ref[0])
        hg_ref[0, rows, :] = y[:, :na]
        pool_ref[0, rows, :] = y[:, na:na + 256]
        cos = cos_ref[rows, :]
        sin = sin_ref[rows, :]
        q = y[:, q0:q0 + qw]
        qn = q * lax.rsqrt(_seg_mean_sq(q, bdq_ref[...]) + EPS) * gq_ref[...]
        cos_q = jnp.concatenate([cos] * (qw // 128), axis=1)
        sin_q = jnp.concatenate([sin] * (qw // 128), axis=1)
        qr = (qn * cos_q + _swap_halves(qn) * sin_q) * (HD ** -0.5 * LOG2E)
        for hh in range(C_HEADS):
            q_ref[0, hh, rows, :] = qr[:, hh * HD:(hh + 1) * HD].astype(BF16)
        k = y[:, q0 + qw:q0 + qw + kw]
        kn = k * lax.rsqrt(_seg_mean_sq(k, bdk_ref[...]) + EPS) * gk_ref[...]
        kr = kn * cos + _swap_halves(kn) * sin
        v = y[:, q0 + qw + kw:q0 + qw + 2 * kw]
        vt = v.T.astype(BF16)
        for hh in range(C_KV_HEADS):
            k_ref[0, hh, rows, :] = kr[:, hh * HD:(hh + 1) * HD].astype(BF16)
            v_ref[0, hh, r] = vt[hh * HD:(hh + 1) * HD, :]
        q2hi, q2lo = _split_bf16(qr * qr)
        qn2 = _dot_nt(selq_ref[...], q2hi) + _dot_nt(selq_ref[...], q2lo)
        for hh in range(C_KV_HEADS):
            qn2_ref[0, hh, :, rows] = qn2[hh * C_GROUP:(hh + 1) * C_GROUP, :]
        k2hi, k2lo = _split_bf16(kr * kr)
        kn2 = _dot(k2hi, bdk_ref[...]) + _dot(k2lo, bdk_ref[...])
        kn2_ref[0, r] = jnp.max(kn2, axis=0, keepdims=True)


def _in_call(xs, moe, mods_prev, mods, g1, w_in, layer, cos_t, sin_t, gq, gk, bdq, bdk, ctx_len):
    bsz, s, d = xs.shape
    tm = TOKEN_TILE
    nsub = TOKEN_NSUB
    bt = nsub * tm
    ncols = w_in.shape[2]
    has_moe = moe is not None

    def tok(b, j):
        return (b, j, 0)

    batch_mod = pl.BlockSpec((1, 1, 8, d), lambda b, j: (0, b, 0, 0))
    ctx_mod = pl.BlockSpec((1, 1, 8, d), lambda b, j: (0, bsz, 0, 0))
    const2 = lambda b, j: (0, 0)
    in_specs = [pl.BlockSpec((1, bt, d), tok)]
    args = [xs]
    if has_moe:
        in_specs += [pl.BlockSpec((MOE_NC, 1, 2, bt, MOE_CW), lambda b, j: (0, b, 0, j, 0)),
                     pl.BlockSpec((1, bt, 128), tok), batch_mod, ctx_mod]
        args += [moe[0], moe[1], mods_prev, mods_prev]
    in_specs += [
        batch_mod,
        ctx_mod,
        pl.BlockSpec((1, d), const2),
        pl.BlockSpec((1, d, ncols), lambda b, j: (layer, 0, 0)),
        pl.BlockSpec((bt, 128), lambda b, j: (j, 0)),
        pl.BlockSpec((bt, 128), lambda b, j: (j, 0)),
        pl.BlockSpec((1, C_HEADS * HD), const2),
        pl.BlockSpec((1, C_KV_HEADS * HD), const2),
        pl.BlockSpec((C_HEADS * HD, C_HEADS * HD), const2),
        pl.BlockSpec((C_KV_HEADS * HD, C_KV_HEADS * HD), const2),
        pl.BlockSpec((C_HEADS, C_HEADS * HD), const2),
    ]
    selq = bdq[::HD]
    args += [mods, mods, g1, w_in, cos_t, sin_t, gq, gk, bdq, bdk, selq]
    out_shape = []
    out_specs = []
    if has_moe:
        out_shape.append(jax.ShapeDtypeStruct((bsz, s, d), F32))
        out_specs.append(pl.BlockSpec((1, bt, d), tok))
    out_shape += [
        jax.ShapeDtypeStruct((bsz, s, 5 * A_WIDTH), F32),
        jax.ShapeDtypeStruct((bsz, s, 256), F32),
        jax.ShapeDtypeStruct((bsz, C_HEADS, s, HD), BF16),
        jax.ShapeDtypeStruct((bsz, C_KV_HEADS, s, HD), BF16),
        jax.ShapeDtypeStruct((bsz, C_KV_HEADS, s // tm, HD, tm), BF16),
        jax.ShapeDtypeStruct((bsz, C_KV_HEADS, C_GROUP, s), F32),
        jax.ShapeDtypeStruct((bsz, s // tm, 1, C_KV_HEADS * HD), F32),
    ]
    out_specs += [
        pl.BlockSpec((1, bt, 5 * A_WIDTH), tok),
        pl.BlockSpec((1, bt, 256), tok),
        pl.BlockSpec((1, C_HEADS, bt, HD), lambda b, j: (b, 0, j, 0)),
        pl.BlockSpec((1, C_KV_HEADS, bt, HD), lambda b, j: (b, 0, j, 0)),
        pl.BlockSpec((1, C_KV_HEADS, nsub, HD, tm), lambda b, j: (b, 0, j, 0, 0)),
        pl.BlockSpec((1, C_KV_HEADS, C_GROUP, bt), lambda b, j: (b, 0, 0, j)),
        pl.BlockSpec((1, nsub, 1, C_KV_HEADS * HD), lambda b, j: (b, j, 0, 0)),
    ]
    return pl.pallas_call(
        functools.partial(_in_kernel, has_moe=has_moe, nsub=nsub, nctx=ctx_len // tm),
        out_shape=out_shape,
        grid=(bsz, s // bt),
        in_specs=in_specs,
        out_specs=out_specs,
        compiler_params=_cparams(("parallel", "arbitrary")),
        name="norm_in_proj",
    )(*args)


def _bcast_rows(x, period, off):
    n, w = x.shape
    if period >= 8:
        g = x.reshape(n // period, period, w)
        return jnp.broadcast_to(g[:, off:off + 1, :], g.shape).reshape(n, w)
    g = x.reshape(n // 8, 8, w)
    sub = lax.broadcasted_iota(jnp.int32, g.shape, 1)
    out = jnp.broadcast_to(g[:, off:off + 1, :], g.shape)
    for i in range(1, 8 // period):
        piece = jnp.broadcast_to(g[:, i * period + off:i * period + off + 1, :], g.shape)
        out = jnp.where(sub >= i * period, piece, out)
    return out.reshape(n, w)


def _hgrn_pair_masks(chunk):
    idx = np.arange(chunk)
    out = np.zeros((2, 1 + int(math.log2(chunk)), chunk, chunk), np.float32)
    for d, tau in enumerate((idx, chunk - 1 - idx)):
        out[d, 0] = np.eye(chunk)
        for level in range(int(math.log2(chunk))):
            c = 1 << level
            later = (tau & c) != 0
            same = (tau[:, None] >> (level + 1)) == (tau[None, :] >> (level + 1))
            out[d, 1 + level] = same & later[:, None] & ~later[None, :]
    return jnp.asarray(out)


def _hgrn_chunk(qz, fz, vz, lb, mask_ref, hm_ref, bd_ref, st_ref, rev, chunk):
    w = A_WIDTH
    row = lax.broadcasted_iota(jnp.int32, (chunk, w), 0)
    tau = (chunk - 1 - row) if rev else row

    q = qz * jax.nn.sigmoid(qz)
    lbf = jnp.maximum(lb, LB_FLOOR)
    e = jnp.exp(-jnp.abs(fz))
    r = 1.0 / (1.0 + e)
    er = e * r
    pos = fz >= 0
    f = lbf + (1.0 - lbf) * jnp.where(pos, r, er)
    kk = (1.0 - lbf) * jnp.where(pos, er, r)
    b = jnp.log2(f)
    step = 1
    while step < chunk:
        b = b + jnp.where(tau >= step, pltpu.roll(b, (chunk - step) if rev else step, axis=0), 0.0)
        step *= 2

    def head_stack(x16):
        return jnp.concatenate([x16 * hm_ref[hh] for hh in range(A_HEADS)], axis=0)

    def masked_pairs(lhs16, rhs16, m):
        p = _dot_nt(lhs16, head_stack(rhs16))
        return jnp.concatenate([p[:, hh * chunk:(hh + 1) * chunk] * m for hh in range(A_HEADS)], axis=1)

    att = masked_pairs(q.astype(BF16), kk.astype(BF16), mask_ref[0])
    for level in range(int(math.log2(chunk))):
        c = 1 << level
        later = (tau & c) != 0
        if level == 0:
            z = jnp.where(later, q * f, kk)
        else:
            mid = _bcast_rows(b, 2 * c, c if rev else c - 1)
            z = jnp.where(later, q, kk) * jnp.exp2(-jnp.abs(b - mid))
        z = z.astype(BF16)
        att = att + masked_pairs(z, z, mask_ref[1 + level])

    b_end = b[0:1, :] if rev else b[chunk - 1:chunk, :]
    qin = (q * jnp.exp2(b)).astype(BF16)
    kend = (kk * jnp.exp2(b_end - b)).astype(BF16)
    vb = vz.astype(BF16)
    st = st_ref[...]
    o = _dot_nt(qin, st.astype(BF16)) + _dot(att.astype(BF16), head_stack(vb))
    st_ref[...] = (st * jnp.exp2(b_end) + _dot_tn(vb, kend)) * bd_ref[...]
    return o


def _hgrn_kernel(qf_ref, ff_ref, vf_ref, qb_ref, fb_ref, vb_ref, lb_ref, mask_ref, hm_ref, bd_ref,
                 of_ref, ob_ref, st_ref, *, chunk):
    @pl.when(pl.program_id(1) == 0)
    def _():
        st_ref[...] = jnp.zeros_like(st_ref)

    of_ref[0] = _hgrn_chunk(qf_ref[0], ff_ref[0], vf_ref[0], lb_ref[0:1, :], mask_ref.at[0], hm_ref, bd_ref,
                            st_ref.at[0], False, chunk)
    ob_ref[0] = _hgrn_chunk(qb_ref[0], fb_ref[0], vb_ref[0], lb_ref[1:2, :], mask_ref.at[1], hm_ref, bd_ref,
                            st_ref.at[1], True, chunk)


def _hgrn_call(hg, lower, ctx_len):
    bsz, s, _ = hg.shape
    chunk = HGRN_CHUNK
    n = s // chunk
    nctx = ctx_len // chunk
    w = A_WIDTH
    masks = _hgrn_pair_masks(chunk)
    head_of_lane = np.arange(w) // HD
    head_masks = jnp.asarray(head_of_lane[None, None, :] == np.arange(A_HEADS)[:, None, None], dtype=BF16)
    block_diag = jnp.asarray(head_of_lane[:, None] == head_of_lane[None, :], dtype=F32)

    def rmap(i):
        return jnp.where(i < nctx, nctx - 1 - i, n + nctx - 1 - i)

    def fwd(col):
        return pl.BlockSpec((1, chunk, w), lambda b, i: (b, i, col))

    def bwd(col):
        return pl.BlockSpec((1, chunk, w), lambda b, i: (b, rmap(i), col))

    return pl.pallas_call(
        functools.partial(_hgrn_kernel, chunk=chunk),
        out_shape=[jax.ShapeDtypeStruct((bsz, s, w), F32)] * 2,
        grid=(bsz, n),
        in_specs=[fwd(0), fwd(1), fwd(3), bwd(0), bwd(2), bwd(3),
                  pl.BlockSpec((2, w), lambda b, i: (0, 0)),
                  pl.BlockSpec(masks.shape, lambda b, i: (0, 0, 0, 0)),
                  pl.BlockSpec((A_HEADS, 1, w), lambda b, i: (0, 0, 0)),
                  pl.BlockSpec((w, w), lambda b, i: (0, 0))],
        out_specs=[pl.BlockSpec((1, chunk, w), lambda b, i: (b, i, 0)),
                   pl.BlockSpec((1, chunk, w), lambda b, i: (b, rmap(i), 0))],
        scratch_shapes=[pltpu.VMEM((2, w, w), F32)],
        compiler_params=_cparams(("parallel", "arbitrary")),
        name="hgrn2_scan",
    )(hg, hg, hg, hg, hg, hg, lower, masks, head_masks, block_diag)


def _attn_kernel(kmax_ref, q_ref, k_ref, vt_ref, qn2_ref, o_ref, *, tq, ts, n_sub, n_ctx_sub, nctx_q, unroll,
                 bounded):
    qi = pl.program_id(2)
    w = C_GROUP * tq
    q = q_ref[0].reshape(w, HD)

    def keys(c0, nc):
        return k_ref[0, 0, pl.ds(pl.multiple_of(c0 * ts, ts), nc * ts), :]

    def values_dot(c0, nc, p):
        pb = p.astype(BF16)
        out = _dot(vt_ref[0, 0, c0], pb[0:ts])
        for c in range(1, nc):
            out = out + _dot(vt_ref[0, 0, c0 + c], pb[c * ts:(c + 1) * ts])
        return out

    if bounded:
        kmax = kmax_ref[pl.program_id(0) * C_KV_HEADS + pl.program_id(1)]
        qn2 = jnp.concatenate([qn2_ref[0, 0, g:g + 1, :] for g in range(C_GROUP)], axis=1)
        shift = jnp.sqrt(qn2) * kmax

        def sub_tile(c0, nc, carry):
            l8, acc = carry
            p = jnp.exp2(_dot_nt(keys(c0, nc), q) - shift)
            l8 = l8 + jnp.sum(p.reshape(nc * ts // 8, 8, w), axis=0)
            return l8, acc + values_dot(c0, nc, p)

        init = (jnp.zeros((8, w), F32), jnp.zeros((HD, w), F32))
    else:
        def sub_tile(c0, nc, carry):
            m_prev, l_prev, acc = carry
            s = _dot_nt(keys(c0, nc), q)
            m_new = jnp.maximum(m_prev, jnp.max(s, axis=0, keepdims=True))
            alpha = jnp.exp2(m_prev - m_new)
            p = jnp.exp2(s - m_new)
            l_new = alpha * l_prev + jnp.sum(p, axis=0, keepdims=True)
            return m_new, l_new, alpha * acc + values_dot(c0, nc, p)

        init = (jnp.full((1, w), -jnp.inf, F32), jnp.zeros((1, w), F32), jnp.zeros((HD, w), F32))

    def run(n):
        carry = init
        if n >= unroll:
            carry = lax.fori_loop(0, n // unroll, lambda i, c: sub_tile(i * unroll, unroll, c), carry)
        if n % unroll:
            carry = sub_tile((n // unroll) * unroll, n % unroll, carry)
        l_fin = jnp.sum(carry[-2], axis=0, keepdims=True)
        ot = carry[-1] / l_fin
        stacked = jnp.concatenate([ot[:, g * tq:(g + 1) * tq] for g in range(C_GROUP)], axis=0)
        o_ref[0] = stacked.T.astype(o_ref.dtype)

    @pl.when(qi < nctx_q)
    def _():
        run(n_ctx_sub)

    @pl.when(qi >= nctx_q)
    def _():
        run(n_sub)


def _attn_call(kmax, q, k, vt, qn2, ctx_len, bounded):
    bsz, _, s, _ = q.shape
    tq = ATTN_TQ
    ts = vt.shape[-1]
    assert ctx_len % ts == 0
    n_sub = s // ts
    grid_spec = pltpu.PrefetchScalarGridSpec(
        num_scalar_prefetch=1,
        grid=(bsz, C_KV_HEADS, s // tq),
        in_specs=[
            pl.BlockSpec((1, C_GROUP, tq, HD), lambda b, g, qi, km: (b, g, qi, 0)),
            pl.BlockSpec((1, 1, s, HD), lambda b, g, qi, km: (b, g, 0, 0)),
            pl.BlockSpec((1, 1, n_sub, HD, ts), lambda b, g, qi, km: (b, g, 0, 0, 0)),
            pl.BlockSpec((1, 1, C_GROUP, tq), lambda b, g, qi, km: (b, g, 0, qi)),
        ],
        out_specs=pl.BlockSpec((1, tq, C_GROUP * HD), lambda b, g, qi, km: (b, qi, g)),
    )
    return pl.pallas_call(
        functools.partial(_attn_kernel, tq=tq, ts=ts, n_sub=n_sub, n_ctx_sub=ctx_len // ts,
                          nctx_q=ctx_len // tq, unroll=ATTN_UNROLL, bounded=bounded),
        out_shape=jax.ShapeDtypeStruct((bsz, s, C_HEADS * HD), BF16),
        grid_spec=grid_spec,
        compiler_params=_cparams(("parallel", "parallel", "arbitrary")),
        name="gqa_attention_bounded" if bounded else "gqa_attention",
    )(kmax, q, k, vt, qn2)


def _attention(q, k, vt, qn2, kn2, ctx_len):
    bsz = q.shape[0]
    k2 = jnp.max(kn2, axis=(1, 2)).reshape(bsz, C_KV_HEADS, HD)[:, :, 0]
    q2 = jnp.max(qn2.reshape(bsz, C_KV_HEADS, -1), axis=-1)
    kmax = (jnp.sqrt(k2) * BOUND_SLACK).reshape(-1)
    small = jnp.max(jnp.sqrt(q2).reshape(-1) * kmax) <= BOUND_MAX
    return lax.cond(small,
                    functools.partial(_attn_call, ctx_len=ctx_len, bounded=True),
                    functools.partial(_attn_call, ctx_len=ctx_len, bounded=False),
                    kmax, q, k, vt, qn2)


def _route(logits_t, bias):
    scores = jax.nn.sigmoid(logits_t)
    sel = scores + bias
    rows = [sel[e:e + 1, :] for e in range(N_EXPERTS)]
    srow = [scores[e:e + 1, :] for e in range(N_EXPERTS)]
    best_val = None
    best_grp = None
    for g in range(N_GROUPS):
        a = rows[g * E_PER_GROUP:(g + 1) * E_PER_GROUP]
        m1 = a[0]
        i1 = jnp.zeros_like(a[0], dtype=jnp.int32)
        for j in range(1, E_PER_GROUP):
            take = a[j] > m1
            m1 = jnp.where(take, a[j], m1)
            i1 = jnp.where(take, j, i1)
        m2 = jnp.full_like(m1, -jnp.inf)
        for j in range(E_PER_GROUP):
            m2 = jnp.where(i1 == j, m2, jnp.maximum(m2, a[j]))
        gs = m1 + m2
        if g == 0:
            best_val, best_grp = gs, jnp.zeros_like(i1)
        else:
            take = gs > best_val
            best_val = jnp.where(take, gs, best_val)
            best_grp = jnp.where(take, g, best_grp)
    masked = [jnp.where(best_grp == (e // E_PER_GROUP), rows[e], NEG_BIG) for e in range(N_EXPERTS)]
    v1 = masked[0]
    e1 = jnp.zeros_like(best_grp)
    for e in range(1, N_EXPERTS):
        take = masked[e] > v1
        v1 = jnp.where(take, masked[e], v1)
        e1 = jnp.where(take, e, e1)
    v2 = jnp.full_like(v1, -jnp.inf)
    e2 = jnp.zeros_like(best_grp)
    for e in range(N_EXPERTS):
        take = jnp.logical_and(e1 != e, masked[e] > v2)
        v2 = jnp.where(take, masked[e], v2)
        e2 = jnp.where(take, e, e2)
    w1 = jnp.zeros_like(v1)
    w2 = jnp.zeros_like(v1)
    for e in range(N_EXPERTS):
        w1 = jnp.where(e1 == e, srow[e], w1)
        w2 = jnp.where(e2 == e, srow[e], w2)
    tot = w1 + w2
    return jnp.concatenate([e1, e2], axis=0), jnp.concatenate([w1 / tot, w2 / tot], axis=0)


def _out_kernel(x_ref, of_ref, ob_ref, g_ref, pc_ref, pp_ref, pn_ref, oc_ref, mod_ref, modc_ref, ng_ref, pw_ref,
                ps_ref, wo_ref, g2_ref, rw_ref, rb_ref, bda_ref,
                xo_ref, h2_ref, idx_ref, wgt_ref, *, nsub, seg_tiles):
    tm = TOKEN_TILE
    nctx, ntot = seg_tiles
    n = tm + 2 * POOL_HALO
    lane = lax.broadcasted_iota(jnp.int32, (tm, 256), 1)
    rhi, rlo = _split_bf16(rw_ref[...])
    for r in range(nsub):
        rows = slice(r * tm, (r + 1) * tm)
        j = pl.program_id(1) * nsub + r
        mod = jnp.where(j < nctx, modc_ref[0, 0], mod_ref[0, 0])
        o = of_ref[0, rows, :] + ob_ref[0, rows, :]
        gz = g_ref[0, rows, :]
        oa = o * lax.rsqrt(_seg_mean_sq(o, bda_ref[...]) + EPS) * ng_ref[...] * (gz * jax.nn.sigmoid(gz))
        first = jnp.logical_or(j == 0, j == nctx)
        last = jnp.logical_or(j == nctx - 1, j == ntot - 1)
        cur = pc_ref[0, rows, :]
        before = pp_ref[0] if r == 0 else pc_ref[0, r * tm - POOL_HALO:r * tm, :]
        after = pn_ref[0] if r == nsub - 1 else pc_ref[0, (r + 1) * tm:(r + 1) * tm + POOL_HALO, :]
        ext = jnp.concatenate([jnp.where(first, 0.0, before), cur, jnp.where(last, 0.0, after)], axis=0)
        s2 = ext + pltpu.roll(ext, 1, axis=0)
        s4 = pltpu.roll(s2, 1, axis=0) + pltpu.roll(s2, n - 1, axis=0)
        s8 = pltpu.roll(s4, 2, axis=0) + pltpu.roll(s4, n - 2, axis=0)
        s16 = pltpu.roll(s8, 4, axis=0) + pltpu.roll(s8, n - 4, axis=0)
        sums = [a[POOL_HALO:POOL_HALO + tm, :] for a in (s2, s4, s8, s16)]
        seg_start = jnp.where(j < nctx, 0, nctx) * tm
        seg_len = jnp.where(j < nctx, nctx, ntot - nctx) * tm
        t = j * tm - seg_start + lax.broadcasted_iota(jnp.int32, (tm, 1), 0)
        mean = jnp.zeros((tm, 256), F32)
        for gi, win in enumerate(POOL_WINDOWS):
            cnt = (jnp.minimum(t + win // 2, seg_len) - jnp.maximum(t - win // 2, 0)).astype(F32)
            mean = jnp.where((lane >> 6) == gi, sums[gi] / cnt, mean)
        ob = _dot((mean - cur).astype(BF16), pw_ref[...]) * ps_ref[...]
        mix = (_dot(oa.astype(BF16), wo_ref[0, 0:256, :]) + _dot(ob.astype(BF16), wo_ref[0, 256:512, :])
               + _dot(oc_ref[0, rows, :], wo_ref[0, 512:1024, :]))
        x = x_ref[0, rows, :] + mod[2:3, :] * mix
        xo_ref[0, rows, :] = x
        ms = jnp.mean(x * x, axis=-1, keepdims=True)
        h2 = x * lax.rsqrt(ms + EPS) * g2_ref[...]
        h2 = h2 * (1.0 + mod[4:5, :]) + mod[3:4, :]
        for c in range(MOE_NC):
            h2_ref[c, rows, :] = h2[:, c * MOE_CW:(c + 1) * MOE_CW]
        hhi, hlo = _split_bf16(h2)
        logits_t = _dot_nt(rhi, hhi) + _dot_nt(rhi, hlo) + _dot_nt(rlo, hhi)
        ids, wts = _route(logits_t, rb_ref[...])
        idx_ref[0, :, rows] = ids
        wgt_ref[0, rows, :] = jnp.concatenate([wts, jnp.zeros((128 - 2, tm), F32)], axis=0).T


def _out_call(xs, of, ob, hg, pool, oc, mods, ng, pw, ps, wo, layer, g2, rw_t, rb, bda, ctx_len):
    bsz, s, d = xs.shape
    tm = TOKEN_TILE
    nsub = TOKEN_NSUB
    bt = nsub * tm
    nt = s // bt
    hb = bt // POOL_HALO
    nhb = s // POOL_HALO

    def tok(b, j):
        return (b, j, 0)

    const2 = lambda b, j: (0, 0)
    return pl.pallas_call(
        functools.partial(_out_kernel, nsub=nsub, seg_tiles=(ctx_len // tm, s // tm)),
        out_shape=[
            jax.ShapeDtypeStruct((bsz, s, d), F32),
            jax.ShapeDtypeStruct((MOE_NC, bsz * s, MOE_CW), F32),
            jax.ShapeDtypeStruct((bsz, 2, s), jnp.int32),
            jax.ShapeDtypeStruct((bsz, s, 128), F32),
        ],
        grid=(bsz, nt),
        in_specs=[
            pl.BlockSpec((1, bt, d), tok),
            pl.BlockSpec((1, bt, A_WIDTH), tok),
            pl.BlockSpec((1, bt, A_WIDTH), tok),
            pl.BlockSpec((1, bt, A_WIDTH), lambda b, j: (b, j, 4)),
            pl.BlockSpec((1, bt, 256), tok),
            pl.BlockSpec((1, POOL_HALO, 256), lambda b, j: (b, jnp.maximum(j * hb - 1, 0), 0)),
            pl.BlockSpec((1, POOL_HALO, 256), lambda b, j: (b, jnp.minimum((j + 1) * hb, nhb - 1), 0)),
            pl.BlockSpec((1, bt, C_HEADS * HD), tok),
            pl.BlockSpec((1, 1, 8, d), lambda b, j: (0, b, 0, 0)),
            pl.BlockSpec((1, 1, 8, d), lambda b, j: (0, bsz, 0, 0)),
            pl.BlockSpec((1, A_WIDTH), const2),
            pl.BlockSpec((256, 256), const2),
            pl.BlockSpec((1, 256), const2),
            pl.BlockSpec((1, d, d), lambda b, j: (layer, 0, 0)),
            pl.BlockSpec((1, d), const2),
            pl.BlockSpec((N_EXPERTS, d), const2),
            pl.BlockSpec((N_EXPERTS, 1), const2),
            pl.BlockSpec((A_WIDTH, A_WIDTH), const2),
        ],
        out_specs=[
            pl.BlockSpec((1, bt, d), tok),
            pl.BlockSpec((MOE_NC, bt, MOE_CW), lambda b, j: (0, b * nt + j, 0)),
            pl.BlockSpec((1, 2, bt), lambda b, j: (b, 0, j)),
            pl.BlockSpec((1, bt, 128), tok),
        ],
        compiler_params=_cparams(("parallel", "arbitrary")),
        name="mix_out_proj_router",
    )(xs, of, ob, hg, pool, pool, pool, oc, mods, mods, ng, pw, ps, wo, g2, rw_t, rb, bda)


def _moe_kernel(be_ref, nb_ref, x_ref, wg_ref, wu_ref, wd_ref, y_ref, wgb, wub, wdb):
    i = pl.program_id(0)
    prev = be_ref[jnp.maximum(i - 1, 0)]
    fresh = jnp.logical_or(i == 0, be_ref[i] != prev)

    @pl.when(fresh)
    def _():
        wgb[...] = wg_ref[0, 0].astype(BF16)
        wub[...] = wu_ref[0, 0].astype(BF16)
        wdb[...] = wd_ref[0, 0].astype(BF16)

    @pl.when(i < nb_ref[0])
    def _():
        x = jnp.concatenate([x_ref[c] for c in range(MOE_NC)], axis=1).astype(BF16)
        gate = _dot(x, wgb[...])
        up = _dot(x, wub[...])
        a = gate * jax.nn.sigmoid(gate) * up
        y = _dot(a.astype(BF16), wdb[...])
        for c in range(MOE_NC):
            y_ref[c] = y[:, c * MOE_CW:(c + 1) * MOE_CW]

    @pl.when(i >= nb_ref[0])
    def _():
        y_ref[...] = jnp.zeros_like(y_ref)


def _moe_call(block_expert, nblocks, xg, wg, wu, wd, layer):
    nc, p, cw = xg.shape
    _, _, d, f = wg.shape
    bm = MOE_BM
    grid_spec = pltpu.PrefetchScalarGridSpec(
        num_scalar_prefetch=2,
        grid=(p // bm,),
        in_specs=[
            pl.BlockSpec((nc, bm, cw), lambda i, be, nb: (0, i, 0)),
            pl.BlockSpec((1, 1, d, f), lambda i, be, nb: (layer, be[i], 0, 0)),
            pl.BlockSpec((1, 1, d, f), lambda i, be, nb: (layer, be[i], 0, 0)),
            pl.BlockSpec((1, 1, f, d), lambda i, be, nb: (layer, be[i], 0, 0)),
        ],
        out_specs=pl.BlockSpec((nc, bm, cw), lambda i, be, nb: (0, i, 0)),
        scratch_shapes=[pltpu.VMEM((d, f), BF16), pltpu.VMEM((d, f), BF16), pltpu.VMEM((f, d), BF16)],
    )
    return pl.pallas_call(
        _moe_kernel,
        out_shape=jax.ShapeDtypeStruct((nc, p, cw), F32),
        grid_spec=grid_spec,
        compiler_params=_cparams(("arbitrary",)),
        name="moe_experts",
    )(block_expert, nblocks, xg, wg, wu, wd)


def _sc_mesh():
    return plsc.VectorSubcoreMesh(core_axis_name="core", subcore_axis_name="subcore")


def _sc_scatter_rows(x, idx, n_out, src_block):
    m = idx.shape[0]

    @pl.kernel(out_type=jax.ShapeDtypeStruct((n_out, x.shape[1]), x.dtype), mesh=_sc_mesh())
    def scatter(x_hbm, i_hbm, o_hbm):
        def body(x_vmem, i_vmem):
            pltpu.sync_copy(x_vmem, o_hbm.at[i_vmem.at[0]])

        pltpu.emit_pipeline(
            body,
            grid=(m // SC_WINDOW,),
            in_specs=[pl.BlockSpec((SC_WINDOW, x.shape[1]), lambda w: (src_block(w), 0)),
                      pl.BlockSpec((1, SC_WINDOW), lambda w: (0, w))],
            out_specs=[],
            core_axis_name=("core", "subcore"),
            dimension_semantics=(pltpu.PARALLEL,),
        )(x_hbm, i_hbm)

    return scatter(x, idx.reshape(1, m))


def _sc_gather_rows(x, idx):
    m = idx.shape[0]

    @pl.kernel(out_type=jax.ShapeDtypeStruct((m, x.shape[1]), x.dtype), mesh=_sc_mesh())
    def gather(x_hbm, i_hbm, o_hbm):
        def body(i_vmem, o_vmem):
            pltpu.sync_copy(x_hbm.at[i_vmem.at[0]], o_vmem)

        pltpu.emit_pipeline(
            body,
            grid=(m // SC_WINDOW,),
            in_specs=[pl.BlockSpec((1, SC_WINDOW), lambda w: (0, w))],
            out_specs=[pl.BlockSpec((SC_WINDOW, x.shape[1]), lambda w: (w, 0))],
            core_axis_name=("core", "subcore"),
            dimension_semantics=(pltpu.PARALLEL,),
        )(i_hbm, o_hbm)

    return gather(x, idx.reshape(1, m))


def _moe_plan(idx, n_tokens):
    bm = MOE_BM
    flat = idx.reshape(-1)
    onehot = (flat[:, None] == jnp.arange(N_EXPERTS)[None, :]).astype(jnp.int32)
    rank = jnp.cumsum(onehot, axis=0) - onehot
    counts = jnp.sum(onehot, axis=0)
    padded = ((counts + bm - 1) // bm) * bm
    seg_end = jnp.cumsum(padded)
    seg_start = seg_end - padded
    slot = jnp.sum(onehot * (seg_start[None, :] + rank), axis=1)
    n_assign = flat.shape[0]
    p = (n_assign // bm + N_EXPERTS) * bm
    nb = p // bm
    blk_start = jnp.arange(nb) * bm
    block_expert = jnp.sum((blk_start[:, None] >= seg_end[None, :]).astype(jnp.int32), axis=1)
    nblocks = (seg_end[-1] // bm).astype(jnp.int32)
    last_e = jnp.max(jnp.where(counts > 0, jnp.arange(N_EXPERTS), 0))
    block_expert = jnp.minimum(block_expert, last_e).astype(jnp.int32)
    return slot, block_expert, nblocks.reshape(1), p


def _final_kernel(x_ref, y_ref, w_ref, mod_ref, o_ref):
    o_ref[0] = x_ref[0] + mod_ref[0, 0, 5:6, :] * _moe_combine(y_ref, w_ref, slice(None))


def _final_call(xs, yc, wcol, mods, ctx_len):
    bsz, s, d = xs.shape
    tm = TOKEN_TILE
    nctx = ctx_len // tm
    t = s - ctx_len
    return pl.pallas_call(
        _final_kernel,
        out_shape=jax.ShapeDtypeStruct((bsz, t, d), F32),
        grid=(bsz, t // tm),
        in_specs=[
            pl.BlockSpec((1, tm, d), lambda b, j: (b, j + nctx, 0)),
            pl.BlockSpec((MOE_NC, 1, 2, tm, MOE_CW), lambda b, j: (0, b, 0, j + nctx, 0)),
            pl.BlockSpec((1, tm, 128), lambda b, j: (b, j + nctx, 0)),
            pl.BlockSpec((1, 1, 8, d), lambda b, j: (0, b, 0, 0)),
        ],
        out_specs=pl.BlockSpec((1, tm, d), lambda b, j: (b, j, 0)),
        compiler_params=_cparams(("parallel", "arbitrary")),
        name="final_residual",
    )(xs, yc, wcol, mods)


def _block_diag_ones(width):
    i = np.arange(width) // HD
    return jnp.asarray((i[:, None] == i[None, :]).astype(np.float32), dtype=BF16)


def _rope_tables(t, ctx_len):
    rows_n = t // GRID_W
    row = jnp.repeat(jnp.arange(rows_n), GRID_W).astype(F32)
    col = jnp.tile(jnp.arange(GRID_W), rows_n).astype(F32)
    n_freq = HD // 4
    inv = ROPE_THETA ** (-jnp.arange(n_freq, dtype=F32) / n_freq)
    ang = jnp.concatenate([row[:, None] * inv, col[:, None] * inv], axis=-1)
    cos, sin = jnp.cos(ang), jnp.sin(ang)
    cos_h = jnp.concatenate([cos, cos], axis=-1)
    sin_h = jnp.concatenate([-sin, sin], axis=-1)
    cos_h = jnp.concatenate([jnp.ones((ctx_len, HD), F32), cos_h], axis=0)
    sin_h = jnp.concatenate([jnp.zeros((ctx_len, HD), F32), sin_h], axis=0)
    return jnp.tile(cos_h, (1, 2)), jnp.tile(sin_h, (1, 2))


def kernel(x, c, ctx, c_ctx, w_mod, b_mod, norm1_g, norm2_g, w_in, w_out, hgrn_lb, hgrn_norm_g,
           pool_w, pool_scale, q_norm_g, k_norm_g, router_w, router_b, moe_w_gate, moe_w_up, moe_w_down):
    bsz, t, d = x.shape
    ctx_len = ctx.shape[1]
    depth = w_mod.shape[0]
    s = ctx_len + t
    assert ctx_len % TOKEN_TILE == 0 and s % (TOKEN_TILE * TOKEN_NSUB) == 0 and t % GRID_W == 0
    assert ctx_len % HGRN_CHUNK == 0 and ctx_len % ATTN_TQ == 0

    perm = np.concatenate([np.arange(0, HD, 2), np.arange(1, HD, 2)])
    na = 5 * A_WIDTH
    q0 = na + 256
    nqk = (C_HEADS + C_KV_HEADS) * HD
    w_qk = w_in[:, :, q0:q0 + nqk].reshape(depth, d, nqk // HD, HD // 2, 2)
    w_qk = jnp.swapaxes(w_qk, -1, -2).reshape(depth, d, nqk)
    w_in_b = jnp.concatenate([w_in[:, :, :q0], w_qk, w_in[:, :, q0 + nqk:]], axis=-1).astype(BF16)
    w_out_b = w_out.astype(BF16)
    gq = jnp.tile(q_norm_g[:, perm], (1, C_HEADS)).reshape(depth, 1, C_HEADS * HD)
    gk = jnp.tile(k_norm_g[:, perm], (1, C_KV_HEADS)).reshape(depth, 1, C_KV_HEADS * HD)
    ng = jnp.tile(hgrn_norm_g, (1, A_HEADS)).reshape(depth, 1, A_WIDTH)
    eye = jnp.eye(B_GROUPS, dtype=F32)
    pw_bd = (eye[None, :, None, :, None] * pool_w[:, :, :, None, :]).reshape(depth, 256, 256).astype(BF16)
    sm = jax.nn.softmax(hgrn_lb.astype(F32), axis=0)
    lower = jnp.cumsum(sm, axis=0) - sm[0:1]
    cos_t, sin_t = _rope_tables(t, ctx_len)
    bdq, bdk, bda = _block_diag_ones(C_HEADS * HD), _block_diag_ones(C_KV_HEADS * HD), _block_diag_ones(A_WIDTH)
    rw_t = router_w.T
    rb = router_b.reshape(N_EXPERTS, 1).astype(F32)

    cond = jnp.zeros((8, d), F32).at[:bsz].set(c).at[bsz].set(c_ctx)
    mods = _mod_call(cond, w_mod, b_mod)
    mods = mods[:, :bsz + 1].reshape(depth, bsz + 1, 6, d)
    mods = jnp.pad(mods, ((0, 0), (0, 0), (0, 2), (0, 0)))

    xs = jnp.concatenate([ctx, x], axis=1)
    moe = None
    n = bsz * s
    wps = s // SC_WINDOW

    def src_block(w):
        j = w % wps
        bk = w // wps
        return (bk // (2 * bsz)) * (n // SC_WINDOW) + ((bk // 2) % bsz) * wps + j

    for l in range(depth):
        outs = _in_call(xs, moe, mods[l - 1:l] if l else None, mods[l:l + 1], norm1_g[l:l + 1], w_in_b, l,
                        cos_t, sin_t, gq[l], gk[l], bdq, bdk, ctx_len)
        if moe is not None:
            xs, outs = outs[0], outs[1:]
        hg, pool, qh, kh, vh, qn2, kn2 = outs
        of, ob = _hgrn_call(hg, lower[l], ctx_len)
        oc = _attention(qh, kh, vh, qn2, kn2, ctx_len)
        xs, h2c, idx, wcol = _out_call(xs, of, ob, hg, pool, oc, mods[l:l + 1], ng[l], pw_bd[l],
                                       pool_scale[l:l + 1], w_out_b, l, norm2_g[l:l + 1], rw_t, rb, bda, ctx_len)
        slot, block_expert, nblocks, p = _moe_plan(idx, n)
        rows = (jnp.arange(MOE_NC, dtype=jnp.int32)[:, None] * p + slot[None, :]).reshape(-1)
        xg = _sc_scatter_rows(h2c.reshape(MOE_NC * n, MOE_CW), rows, MOE_NC * p, src_block)
        ys = _moe_call(block_expert, nblocks, xg.reshape(MOE_NC, p, MOE_CW),
                       moe_w_gate, moe_w_up, moe_w_down, l)
        yc = _sc_gather_rows(ys.reshape(MOE_NC * p, MOE_CW), rows)
        moe = (yc.reshape(MOE_NC, bsz, 2, s, MOE_CW), wcol)
    return _final_call(xs, moe[0], moe[1], mods[depth - 1:depth], ctx_len)
```

```python
import functools
import math

import jax
import jax.numpy as jnp
import numpy as np
from jax import lax
from jax.experimental import pallas as pl
from jax.experimental.pallas import tpu as pltpu
from jax.experimental.pallas import tpu_sc as plsc

F32 = jnp.float32
BF16 = jnp.bfloat16

EPS = 1e-6
LB_FLOOR = 1e-30
NEG_BIG = -1e4
GRID_W = 64
ROPE_THETA = 10000.0
HD = 64
A_HEADS = 4
A_WIDTH = A_HEADS * HD
B_GROUPS = 4
POOL_WINDOWS = (2, 4, 8, 16)
POOL_HALO = 8
C_HEADS = 8
C_KV_HEADS = 2
C_GROUP = C_HEADS // C_KV_HEADS
N_EXPERTS = 16
N_GROUPS = 4
E_PER_GROUP = N_EXPERTS // N_GROUPS

TOKEN_TILE = 256
TOKEN_NSUB = 3
HGRN_CHUNK = 128
ATTN_TQ = 256
ATTN_UNROLL = 33
LOG2E = 1.4426950408889634
BOUND_SLACK = 1.02
BOUND_MAX = 48.0
MOE_BM = 512
MOE_NC = 2
MOE_CW = 256
SC_WINDOW = 128
VMEM_LIMIT = 56 * 1024 * 1024


def _cparams(sem):
    return pltpu.CompilerParams(dimension_semantics=sem, vmem_limit_bytes=VMEM_LIMIT)


def _dot(a, b):
    return jnp.dot(a, b, preferred_element_type=F32)


def _dot_nt(a, b):
    return lax.dot_general(a, b, (((1,), (1,)), ((), ())), preferred_element_type=F32)


def _dot_tn(a, b):
    return lax.dot_general(a, b, (((0,), (0,)), ((), ())), preferred_element_type=F32)


def _split_bf16(x):
    hi = x.astype(BF16)
    lo = (x - hi.astype(F32)).astype(BF16)
    return hi, lo


def _seg_mean_sq(x, bd):
    hi, lo = _split_bf16(x * x)
    return (_dot(hi, bd) + _dot(lo, bd)) * (1.0 / HD)


def _swap_halves(x):
    w = x.shape[-1]
    lane = lax.broadcasted_iota(jnp.int32, x.shape, x.ndim - 1)
    up = pltpu.roll(x, w - HD // 2, axis=x.ndim - 1)
    dn = pltpu.roll(x, HD // 2, axis=x.ndim - 1)
    return jnp.where((lane & (HD // 2)) == 0, up, dn)


def _mod_kernel(c_ref, w_ref, b_ref, o_ref):
    c = c_ref[...]
    a = c * jax.nn.sigmoid(c)
    hi, lo = _split_bf16(a)
    whi, wlo = _split_bf16(w_ref[0])
    o_ref[0] = _dot(hi, whi) + _dot(lo, whi) + _dot(hi, wlo) + b_ref[0]


def _mod_call(cond, w_mod, b_mod):
    depth, d, n6 = w_mod.shape
    tn = 1536
    return pl.pallas_call(
        _mod_kernel,
        out_shape=jax.ShapeDtypeStruct((depth, 8, n6), F32),
        grid=(depth, n6 // tn),
        in_specs=[
            pl.BlockSpec((8, d), lambda l, j: (0, 0)),
            pl.BlockSpec((1, d, tn), lambda l, j: (l, 0, j)),
            pl.BlockSpec((1, 1, tn), lambda l, j: (l, 0, j)),
        ],
        out_specs=pl.BlockSpec((1, 8, tn), lambda l, j: (l, 0, j)),
        compiler_params=_cparams(("arbitrary", "arbitrary")),
        name="adaln_mod",
    )(cond, w_mod, b_mod.reshape(depth, 1, n6))


def _pack_bf16_pairs(x):
    half = x.shape[1] // 2
    bits = lax.bitcast_convert_type(x.astype(BF16).astype(F32), jnp.uint32)
    return (bits[:, :half] >> 16) | (bits[:, half:] & jnp.uint32(0xFFFF0000))


def _unpack_bf16_pairs(words):
    lo = lax.bitcast_convert_type(words << 16, F32)
    hi = lax.bitcast_convert_type(words & jnp.uint32(0xFFFF0000), F32)
    return jnp.concatenate([lo, hi], axis=1)


def _store_planes(ref, lead, words):
    for c in range(MOE_NC):
        ref[(c,) + lead] = words[:, c * MOE_CW:(c + 1) * MOE_CW]


def _load_planes(ref, lead):
    return jnp.concatenate([ref[(c,) + lead] for c in range(MOE_NC)], axis=1)


def _moe_combine(y_ref, w_ref, rows):
    y0 = _unpack_bf16_pairs(_load_planes(y_ref, (0, 0, rows, slice(None))))
    y1 = _unpack_bf16_pairs(_load_planes(y_ref, (0, 1, rows, slice(None))))
    return w_ref[0, rows, 0:1] * y0 + w_ref[0, rows, 1:2] * y1


def _in_kernel(*refs, has_moe, nsub, nctx):
    if has_moe:
        (x_ref, y_ref, wc_ref, modp_ref, modpc_ref, mod_ref, modc_ref, g_ref, w_ref, cos_ref, sin_ref, gq_ref,
         gk_ref, bdq_ref, bdk_ref, selq_ref, xo_ref, hg_ref, pool_ref, q_ref, k_ref, v_ref, qn2_ref,
         kn2_ref) = refs
    else:
        (x_ref, mod_ref, modc_ref, g_ref, w_ref, cos_ref, sin_ref, gq_ref, gk_ref,
         bdq_ref, bdk_ref, selq_ref, hg_ref, pool_ref, q_ref, k_ref, v_ref, qn2_ref, kn2_ref) = refs
    tm = TOKEN_TILE
    na = 5 * A_WIDTH
    q0 = na + 256
    qw = C_HEADS * HD
    kw = C_KV_HEADS * HD
    for r in range(nsub):
        rows = slice(r * tm, (r + 1) * tm)
        is_ctx = pl.program_id(1) * nsub + r < nctx
        mod = jnp.where(is_ctx, modc_ref[0, 0], mod_ref[0, 0])
        x = x_ref[0, rows, :]
        if has_moe:
            ga2 = jnp.where(is_ctx, modpc_ref[0, 0, 5:6, :], modp_ref[0, 0, 5:6, :])
            x = x + ga2 * _moe_combine(y_ref, wc_ref, rows)
            xo_ref[0, rows, :] = x
        ms = jnp.mean(x * x, axis=-1, keepdims=True)
        h = x * lax.rsqrt(ms + EPS) * g_ref[...]
        h = h * (1.0 + mod[1:2, :]) + mod[0:1, :]
        y = _dot(h.astype(BF16), w_ref[0])
        hg_ref[0, rows, :] = y[:, :na]
        pool_ref[0, rows, :] = y[:, na:na + 256]
        cos = cos_ref[rows, :]
        sin = sin_ref[rows, :]
        q = y[:, q0:q0 + qw]
        qn = q * lax.rsqrt(_seg_mean_sq(q, bdq_ref[...]) + EPS) * gq_ref[...]
        cos_q = jnp.concatenate([cos] * (qw // 128), axis=1)
        sin_q = jnp.concatenate([sin] * (qw // 128), axis=1)
        qr = (qn * cos_q + _swap_halves(qn) * sin_q) * (HD ** -0.5 * LOG2E)
        for hh in range(C_HEADS):
            q_ref[0, hh, rows, :] = qr[:, hh * HD:(hh + 1) * HD].astype(BF16)
        k = y[:, q0 + qw:q0 + qw + kw]
        kn = k * lax.rsqrt(_seg_mean_sq(k, bdk_ref[...]) + EPS) * gk_ref[...]
        kr = kn * cos + _swap_halves(kn) * sin
        v = y[:, q0 + qw + kw:q0 + qw + 2 * kw]
        vt = v.T.astype(BF16)
        for hh in range(C_KV_HEADS):
            k_ref[0, hh, rows, :] = kr[:, hh * HD:(hh + 1) * HD].astype(BF16)
            v_ref[0, hh, r] = vt[hh * HD:(hh + 1) * HD, :]
        q2hi, q2lo = _split_bf16(qr * qr)
        qn2 = _dot_nt(selq_ref[...], q2hi) + _dot_nt(selq_ref[...], q2lo)
        for hh in range(C_KV_HEADS):
            qn2_ref[0, hh, :, rows] = qn2[hh * C_GROUP:(hh + 1) * C_GROUP, :]
        k2hi, k2lo = _split_bf16(kr * kr)
        kn2 = _dot(k2hi, bdk_ref[...]) + _dot(k2lo, bdk_ref[...])
        kn2_ref[0, r] = jnp.max(kn2, axis=0, keepdims=True)


def _in_call(xs, moe, mods_prev, mods, g1, w_in, layer, cos_t, sin_t, gq, gk, bdq, bdk, ctx_len):
    bsz, s, d = xs.shape
    tm = TOKEN_TILE
    nsub = TOKEN_NSUB
    bt = nsub * tm
    ncols = w_in.shape[2]
    has_moe = moe is not None

    def tok(b, j):
        return (b, j, 0)

    batch_mod = pl.BlockSpec((1, 1, 8, d), lambda b, j: (0, b, 0, 0))
    ctx_mod = pl.BlockSpec((1, 1, 8, d), lambda b, j: (0, bsz, 0, 0))
    const2 = lambda b, j: (0, 0)
    in_specs = [pl.BlockSpec((1, bt, d), tok)]
    args = [xs]
    if has_moe:
        in_specs += [pl.BlockSpec((MOE_NC, 1, 2, bt, MOE_CW), lambda b, j: (0, b, 0, j, 0)),
                     pl.BlockSpec((1, bt, 128), tok), batch_mod, ctx_mod]
        args += [moe[0], moe[1], mods_prev, mods_prev]
    in_specs += [
        batch_mod,
        ctx_mod,
        pl.BlockSpec((1, d), const2),
        pl.BlockSpec((1, d, ncols), lambda b, j: (layer, 0, 0)),
        pl.BlockSpec((bt, 128), lambda b, j: (j, 0)),
        pl.BlockSpec((bt, 128), lambda b, j: (j, 0)),
        pl.BlockSpec((1, C_HEADS * HD), const2),
        pl.BlockSpec((1, C_KV_HEADS * HD), const2),
        pl.BlockSpec((C_HEADS * HD, C_HEADS * HD), const2),
        pl.BlockSpec((C_KV_HEADS * HD, C_KV_HEADS * HD), const2),
        pl.BlockSpec((C_HEADS, C_HEADS * HD), const2),
    ]
    selq = bdq[::HD]
    args += [mods, mods, g1, w_in, cos_t, sin_t, gq, gk, bdq, bdk, selq]
    out_shape = []
    out_specs = []
    if has_moe:
        out_shape.append(jax.ShapeDtypeStruct((bsz, s, d), F32))
        out_specs.append(pl.BlockSpec((1, bt, d), tok))
    out_shape += [
        jax.ShapeDtypeStruct((bsz, s, 5 * A_WIDTH), F32),
        jax.ShapeDtypeStruct((bsz, s, 256), F32),
        jax.ShapeDtypeStruct((bsz, C_HEADS, s, HD), BF16),
        jax.ShapeDtypeStruct((bsz, C_KV_HEADS, s, HD), BF16),
        jax.ShapeDtypeStruct((bsz, C_KV_HEADS, s // tm, HD, tm), BF16),
        jax.ShapeDtypeStruct((bsz, C_KV_HEADS, C_GROUP, s), F32),
        jax.ShapeDtypeStruct((bsz, s // tm, 1, C_KV_HEADS * HD), F32),
    ]
    out_specs += [
        pl.BlockSpec((1, bt, 5 * A_WIDTH), tok),
        pl.BlockSpec((1, bt, 256), tok),
        pl.BlockSpec((1, C_HEADS, bt, HD), lambda b, j: (b, 0, j, 0)),
        pl.BlockSpec((1, C_KV_HEADS, bt, HD), lambda b, j: (b, 0, j, 0)),
        pl.BlockSpec((1, C_KV_HEADS, nsub, HD, tm), lambda b, j: (b, 0, j, 0, 0)),
        pl.BlockSpec((1, C_KV_HEADS, C_GROUP, bt), lambda b, j: (b, 0, 0, j)),
        pl.BlockSpec((1, nsub, 1, C_KV_HEADS * HD), lambda b, j: (b, j, 0, 0)),
    ]
    return pl.pallas_call(
        functools.partial(_in_kernel, has_moe=has_moe, nsub=nsub, nctx=ctx_len // tm),
        out_shape=out_shape,
        grid=(bsz, s // bt),
        in_specs=in_specs,
        out_specs=out_specs,
        compiler_params=_cparams(("parallel", "arbitrary")),
        name="norm_in_proj",
    )(*args)


def _bcast_rows(x, period, off):
    n, w = x.shape
    if period >= 8:
        g = x.reshape(n // period, period, w)
        return jnp.broadcast_to(g[:, off:off + 1, :], g.shape).reshape(n, w)
    g = x.reshape(n // 8, 8, w)
    sub = lax.broadcasted_iota(jnp.int32, g.shape, 1)
    out = jnp.broadcast_to(g[:, off:off + 1, :], g.shape)
    for i in range(1, 8 // period):
        piece = jnp.broadcast_to(g[:, i * period + off:i * period + off + 1, :], g.shape)
        out = jnp.where(sub >= i * period, piece, out)
    return out.reshape(n, w)


def _hgrn_pair_masks(chunk):
    idx = np.arange(chunk)
    out = np.zeros((2, 1 + int(math.log2(chunk)), chunk, chunk), np.float32)
    for d, tau in enumerate((idx, chunk - 1 - idx)):
        out[d, 0] = np.eye(chunk)
        for level in range(int(math.log2(chunk))):
            c = 1 << level
            later = (tau & c) != 0
            same = (tau[:, None] >> (level + 1)) == (tau[None, :] >> (level + 1))
            out[d, 1 + level] = same & later[:, None] & ~later[None, :]
    return jnp.asarray(out)


def _hgrn_chunk(qz, fz, vz, lb, mask_ref, hm_ref, bd_ref, st_ref, rev, chunk):
    w = A_WIDTH
    row = lax.broadcasted_iota(jnp.int32, (chunk, w), 0)
    tau = (chunk - 1 - row) if rev else row

    q = qz * jax.nn.sigmoid(qz)
    lbf = jnp.maximum(lb, LB_FLOOR)
    e = jnp.exp(-jnp.abs(fz))
    r = 1.0 / (1.0 + e)
    er = e * r
    pos = fz >= 0
    f = lbf + (1.0 - lbf) * jnp.where(pos, r, er)
    kk = (1.0 - lbf) * jnp.where(pos, er, r)
    b = jnp.log2(f)
    step = 1
    while step < chunk:
        b = b + jnp.where(tau >= step, pltpu.roll(b, (chunk - step) if rev else step, axis=0), 0.0)
        step *= 2

    def head_stack(x16):
        return jnp.concatenate([x16 * hm_ref[hh] for hh in range(A_HEADS)], axis=0)

    def masked_pairs(lhs16, rhs16, m):
        p = _dot_nt(lhs16, head_stack(rhs16))
        return jnp.concatenate([p[:, hh * chunk:(hh + 1) * chunk] * m for hh in range(A_HEADS)], axis=1)

    att = masked_pairs(q.astype(BF16), kk.astype(BF16), mask_ref[0])
    for level in range(int(math.log2(chunk))):
        c = 1 << level
        later = (tau & c) != 0
        if level == 0:
            z = jnp.where(later, q * f, kk)
        else:
            mid = _bcast_rows(b, 2 * c, c if rev else c - 1)
            z = jnp.where(later, q, kk) * jnp.exp2(-jnp.abs(b - mid))
        z = z.astype(BF16)
        att = att + masked_pairs(z, z, mask_ref[1 + level])

    b_end = b[0:1, :] if rev else b[chunk - 1:chunk, :]
    qin = (q * jnp.exp2(b)).astype(BF16)
    kend = (kk * jnp.exp2(b_end - b)).astype(BF16)
    vb = vz.astype(BF16)
    st = st_ref[...]
    o = _dot_nt(qin, st.astype(BF16)) + _dot(att.astype(BF16), head_stack(vb))
    st_ref[...] = (st * jnp.exp2(b_end) + _dot_tn(vb, kend)) * bd_ref[...]
    return o


def _hgrn_kernel(qf_ref, ff_ref, vf_ref, qb_ref, fb_ref, vb_ref, lb_ref, mask_ref, hm_ref, bd_ref,
                 of_ref, ob_ref, st_ref, *, chunk):
    @pl.when(pl.program_id(1) == 0)
    def _():
        st_ref[...] = jnp.zeros_like(st_ref)

    of_ref[0] = _hgrn_chunk(qf_ref[0], ff_ref[0], vf_ref[0], lb_ref[0:1, :], mask_ref.at[0], hm_ref, bd_ref,
                            st_ref.at[0], False, chunk)
    ob_ref[0] = _hgrn_chunk(qb_ref[0], fb_ref[0], vb_ref[0], lb_ref[1:2, :], mask_ref.at[1], hm_ref, bd_ref,
                            st_ref.at[1], True, chunk)


def _hgrn_call(hg, lower, ctx_len):
    bsz, s, _ = hg.shape
    chunk = HGRN_CHUNK
    n = s // chunk
    nctx = ctx_len // chunk
    w = A_WIDTH
    masks = _hgrn_pair_masks(chunk)
    head_of_lane = np.arange(w) // HD
    head_masks = jnp.asarray(head_of_lane[None, None, :] == np.arange(A_HEADS)[:, None, None], dtype=BF16)
    block_diag = jnp.asarray(head_of_lane[:, None] == head_of_lane[None, :], dtype=F32)

    def rmap(i):
        return jnp.where(i < nctx, nctx - 1 - i, n + nctx - 1 - i)

    def fwd(col):
        return pl.BlockSpec((1, chunk, w), lambda b, i: (b, i, col))

    def bwd(col):
        return pl.BlockSpec((1, chunk, w), lambda b, i: (b, rmap(i), col))

    return pl.pallas_call(
        functools.partial(_hgrn_kernel, chunk=chunk),
        out_shape=[jax.ShapeDtypeStruct((bsz, s, w), F32)] * 2,
        grid=(bsz, n),
        in_specs=[fwd(0), fwd(1), fwd(3), bwd(0), bwd(2), bwd(3),
                  pl.BlockSpec((2, w), lambda b, i: (0, 0)),
                  pl.BlockSpec(masks.shape, lambda b, i: (0, 0, 0, 0)),
                  pl.BlockSpec((A_HEADS, 1, w), lambda b, i: (0, 0, 0)),
                  pl.BlockSpec((w, w), lambda b, i: (0, 0))],
        out_specs=[pl.BlockSpec((1, chunk, w), lambda b, i: (b, i, 0)),
                   pl.BlockSpec((1, chunk, w), lambda b, i: (b, rmap(i), 0))],
        scratch_shapes=[pltpu.VMEM((2, w, w), F32)],
        compiler_params=_cparams(("parallel", "arbitrary")),
        name="hgrn2_scan",
    )(hg, hg, hg, hg, hg, hg, lower, masks, head_masks, block_diag)


def _attn_kernel(kmax_ref, q_ref, k_ref, vt_ref, qn2_ref, o_ref, *, tq, ts, n_sub, n_ctx_sub, nctx_q, unroll,
                 bounded):
    qi = pl.program_id(2)
    w = C_GROUP * tq
    q = q_ref[0].reshape(w, HD)

    def keys(c0, nc):
        return k_ref[0, 0, pl.ds(pl.multiple_of(c0 * ts, ts), nc * ts), :]

    def values_dot(c0, nc, p):
        pb = p.astype(BF16)
        out = _dot(vt_ref[0, 0, c0], pb[0:ts])
        for c in range(1, nc):
            out = out + _dot(vt_ref[0, 0, c0 + c], pb[c * ts:(c + 1) * ts])
        return out

    if bounded:
        kmax = kmax_ref[pl.program_id(0) * C_KV_HEADS + pl.program_id(1)]
        qn2 = jnp.concatenate([qn2_ref[0, 0, g:g + 1, :] for g in range(C_GROUP)], axis=1)
        shift = jnp.sqrt(qn2) * kmax

        def sub_tile(c0, nc, carry):
            l8, acc = carry
            p = jnp.exp2(_dot_nt(keys(c0, nc), q) - shift)
            l8 = l8 + jnp.sum(p.reshape(nc * ts // 8, 8, w), axis=0)
            return l8, acc + values_dot(c0, nc, p)

        init = (jnp.zeros((8, w), F32), jnp.zeros((HD, w), F32))
    else:
        def sub_tile(c0, nc, carry):
            m_prev, l_prev, acc = carry
            s = _dot_nt(keys(c0, nc), q)
            m_new = jnp.maximum(m_prev, jnp.max(s, axis=0, keepdims=True))
            alpha = jnp.exp2(m_prev - m_new)
            p = jnp.exp2(s - m_new)
            l_new = alpha * l_prev + jnp.sum(p, axis=0, keepdims=True)
            return m_new, l_new, alpha * acc + values_dot(c0, nc, p)

        init = (jnp.full((1, w), -jnp.inf, F32), jnp.zeros((1, w), F32), jnp.zeros((HD, w), F32))

    def run(n):
        carry = init
        if n >= unroll:
            carry = lax.fori_loop(0, n // unroll, lambda i, c: sub_tile(i * unroll, unroll, c), carry)
        if n % unroll:
            carry = sub_tile((n // unroll) * unroll, n % unroll, carry)
        l_fin = jnp.sum(carry[-2], axis=0, keepdims=True)
        ot = carry[-1] / l_fin
        stacked = jnp.concatenate([ot[:, g * tq:(g + 1) * tq] for g in range(C_GROUP)], axis=0)
        o_ref[0] = stacked.T.astype(o_ref.dtype)

    @pl.when(qi < nctx_q)
    def _():
        run(n_ctx_sub)

    @pl.when(qi >= nctx_q)
    def _():
        run(n_sub)


def _attn_call(kmax, q, k, vt, qn2, ctx_len, bounded):
    bsz, _, s, _ = q.shape
    tq = ATTN_TQ
    ts = vt.shape[-1]
    assert ctx_len % ts == 0
    n_sub = s // ts
    grid_spec = pltpu.PrefetchScalarGridSpec(
        num_scalar_prefetch=1,
        grid=(bsz, C_KV_HEADS, s // tq),
        in_specs=[
            pl.BlockSpec((1, C_GROUP, tq, HD), lambda b, g, qi, km: (b, g, qi, 0)),
            pl.BlockSpec((1, 1, s, HD), lambda b, g, qi, km: (b, g, 0, 0)),
            pl.BlockSpec((1, 1, n_sub, HD, ts), lambda b, g, qi, km: (b, g, 0, 0, 0)),
            pl.BlockSpec((1, 1, C_GROUP, tq), lambda b, g, qi, km: (b, g, 0, qi)),
        ],
        out_specs=pl.BlockSpec((1, tq, C_GROUP * HD), lambda b, g, qi, km: (b, qi, g)),
    )
    return pl.pallas_call(
        functools.partial(_attn_kernel, tq=tq, ts=ts, n_sub=n_sub, n_ctx_sub=ctx_len // ts,
                          nctx_q=ctx_len // tq, unroll=ATTN_UNROLL, bounded=bounded),
        out_shape=jax.ShapeDtypeStruct((bsz, s, C_HEADS * HD), BF16),
        grid_spec=grid_spec,
        compiler_params=_cparams(("parallel", "parallel", "arbitrary")),
        name="gqa_attention_bounded" if bounded else "gqa_attention",
    )(kmax, q, k, vt, qn2)


def _attention(q, k, vt, qn2, kn2, ctx_len):
    bsz = q.shape[0]
    k2 = jnp.max(kn2, axis=(1, 2)).reshape(bsz, C_KV_HEADS, HD)[:, :, 0]
    q2 = jnp.max(qn2.reshape(bsz, C_KV_HEADS, -1), axis=-1)
    kmax = (jnp.sqrt(k2) * BOUND_SLACK).reshape(-1)
    small = jnp.max(jnp.sqrt(q2).reshape(-1) * kmax) <= BOUND_MAX
    return lax.cond(small,
                    functools.partial(_attn_call, ctx_len=ctx_len, bounded=True),
                    functools.partial(_attn_call, ctx_len=ctx_len, bounded=False),
                    kmax, q, k, vt, qn2)


def _route(logits_t, bias):
    scores = jax.nn.sigmoid(logits_t)
    sel = scores + bias
    rows = [sel[e:e + 1, :] for e in range(N_EXPERTS)]
    srow = [scores[e:e + 1, :] for e in range(N_EXPERTS)]
    best_val = None
    best_grp = None
    for g in range(N_GROUPS):
        a = rows[g * E_PER_GROUP:(g + 1) * E_PER_GROUP]
        m1 = a[0]
        i1 = jnp.zeros_like(a[0], dtype=jnp.int32)
        for j in range(1, E_PER_GROUP):
            take = a[j] > m1
            m1 = jnp.where(take, a[j], m1)
            i1 = jnp.where(take, j, i1)
        m2 = jnp.full_like(m1, -jnp.inf)
        for j in range(E_PER_GROUP):
            m2 = jnp.where(i1 == j, m2, jnp.maximum(m2, a[j]))
        gs = m1 + m2
        if g == 0:
            best_val, best_grp = gs, jnp.zeros_like(i1)
        else:
            take = gs > best_val
            best_val = jnp.where(take, gs, best_val)
            best_grp = jnp.where(take, g, best_grp)
    masked = [jnp.where(best_grp == (e // E_PER_GROUP), rows[e], NEG_BIG) for e in range(N_EXPERTS)]
    v1 = masked[0]
    e1 = jnp.zeros_like(best_grp)
    for e in range(1, N_EXPERTS):
        take = masked[e] > v1
        v1 = jnp.where(take, masked[e], v1)
        e1 = jnp.where(take, e, e1)
    v2 = jnp.full_like(v1, -jnp.inf)
    e2 = jnp.zeros_like(best_grp)
    for e in range(N_EXPERTS):
        take = jnp.logical_and(e1 != e, masked[e] > v2)
        v2 = jnp.where(take, masked[e], v2)
        e2 = jnp.where(take, e, e2)
    w1 = jnp.zeros_like(v1)
    w2 = jnp.zeros_like(v1)
    for e in range(N_EXPERTS):
        w1 = jnp.where(e1 == e, srow[e], w1)
        w2 = jnp.where(e2 == e, srow[e], w2)
    tot = w1 + w2
    return jnp.concatenate([e1, e2], axis=0), jnp.concatenate([w1 / tot, w2 / tot], axis=0)


def _out_kernel(x_ref, of_ref, ob_ref, g_ref, pc_ref, pp_ref, pn_ref, oc_ref, mod_ref, modc_ref, ng_ref, pw_ref,
                ps_ref, wo_ref, g2_ref, rw_ref, rb_ref, bda_ref,
                xo_ref, h2_ref, idx_ref, wgt_ref, *, nsub, seg_tiles):
    tm = TOKEN_TILE
    nctx, ntot = seg_tiles
    n = tm + 2 * POOL_HALO
    lane = lax.broadcasted_iota(jnp.int32, (tm, 256), 1)
    rhi, rlo = _split_bf16(rw_ref[...])
    for r in range(nsub):
        rows = slice(r * tm, (r + 1) * tm)
        j = pl.program_id(1) * nsub + r
        mod = jnp.where(j < nctx, modc_ref[0, 0], mod_ref[0, 0])
        o = of_ref[0, rows, :] + ob_ref[0, rows, :]
        gz = g_ref[0, rows, :]
        oa = o * lax.rsqrt(_seg_mean_sq(o, bda_ref[...]) + EPS) * ng_ref[...] * (gz * jax.nn.sigmoid(gz))
        first = jnp.logical_or(j == 0, j == nctx)
        last = jnp.logical_or(j == nctx - 1, j == ntot - 1)
        cur = pc_ref[0, rows, :]
        before = pp_ref[0] if r == 0 else pc_ref[0, r * tm - POOL_HALO:r * tm, :]
        after = pn_ref[0] if r == nsub - 1 else pc_ref[0, (r + 1) * tm:(r + 1) * tm + POOL_HALO, :]
        ext = jnp.concatenate([jnp.where(first, 0.0, before), cur, jnp.where(last, 0.0, after)], axis=0)
        s2 = ext + pltpu.roll(ext, 1, axis=0)
        s4 = pltpu.roll(s2, 1, axis=0) + pltpu.roll(s2, n - 1, axis=0)
        s8 = pltpu.roll(s4, 2, axis=0) + pltpu.roll(s4, n - 2, axis=0)
        s16 = pltpu.roll(s8, 4, axis=0) + pltpu.roll(s8, n - 4, axis=0)
        sums = [a[POOL_HALO:POOL_HALO + tm, :] for a in (s2, s4, s8, s16)]
        seg_start = jnp.where(j < nctx, 0, nctx) * tm
        seg_len = jnp.where(j < nctx, nctx, ntot - nctx) * tm
        t = j * tm - seg_start + lax.broadcasted_iota(jnp.int32, (tm, 1), 0)
        mean = jnp.zeros((tm, 256), F32)
        for gi, win in enumerate(POOL_WINDOWS):
            cnt = (jnp.minimum(t + win // 2, seg_len) - jnp.maximum(t - win // 2, 0)).astype(F32)
            mean = jnp.where((lane >> 6) == gi, sums[gi] / cnt, mean)
        ob = _dot((mean - cur).astype(BF16), pw_ref[...]) * ps_ref[...]
        mix = (_dot(oa.astype(BF16), wo_ref[0, 0:256, :]) + _dot(ob.astype(BF16), wo_ref[0, 256:512, :])
               + _dot(oc_ref[0, rows, :], wo_ref[0, 512:1024, :]))
        x = x_ref[0, rows, :] + mod[2:3, :] * mix
        xo_ref[0, rows, :] = x
        ms = jnp.mean(x * x, axis=-1, keepdims=True)
        h2 = x * lax.rsqrt(ms + EPS) * g2_ref[...]
        h2 = h2 * (1.0 + mod[4:5, :]) + mod[3:4, :]
        _store_planes(h2_ref, (rows, slice(None)), _pack_bf16_pairs(h2))
        hhi, hlo = _split_bf16(h2)
        logits_t = _dot_nt(rhi, hhi) + _dot_nt(rhi, hlo) + _dot_nt(rlo, hhi)
        ids, wts = _route(logits_t, rb_ref[...])
        idx_ref[0, :, rows] = ids
        wgt_ref[0, rows, :] = jnp.concatenate([wts, jnp.zeros((128 - 2, tm), F32)], axis=0).T


def _out_call(xs, of, ob, hg, pool, oc, mods, ng, pw, ps, wo, layer, g2, rw_t, rb, bda, ctx_len):
    bsz, s, d = xs.shape
    tm = TOKEN_TILE
    nsub = TOKEN_NSUB
    bt = nsub * tm
    nt = s // bt
    hb = bt // POOL_HALO
    nhb = s // POOL_HALO

    def tok(b, j):
        return (b, j, 0)

    const2 = lambda b, j: (0, 0)
    return pl.pallas_call(
        functools.partial(_out_kernel, nsub=nsub, seg_tiles=(ctx_len // tm, s // tm)),
        out_shape=[
            jax.ShapeDtypeStruct((bsz, s, d), F32),
            jax.ShapeDtypeStruct((MOE_NC, bsz * s, MOE_CW), jnp.uint32),
            jax.ShapeDtypeStruct((bsz, 2, s), jnp.int32),
            jax.ShapeDtypeStruct((bsz, s, 128), F32),
        ],
        grid=(bsz, nt),
        in_specs=[
            pl.BlockSpec((1, bt, d), tok),
            pl.BlockSpec((1, bt, A_WIDTH), tok),
            pl.BlockSpec((1, bt, A_WIDTH), tok),
            pl.BlockSpec((1, bt, A_WIDTH), lambda b, j: (b, j, 4)),
            pl.BlockSpec((1, bt, 256), tok),
            pl.BlockSpec((1, POOL_HALO, 256), lambda b, j: (b, jnp.maximum(j * hb - 1, 0), 0)),
            pl.BlockSpec((1, POOL_HALO, 256), lambda b, j: (b, jnp.minimum((j + 1) * hb, nhb - 1), 0)),
            pl.BlockSpec((1, bt, C_HEADS * HD), tok),
            pl.BlockSpec((1, 1, 8, d), lambda b, j: (0, b, 0, 0)),
            pl.BlockSpec((1, 1, 8, d), lambda b, j: (0, bsz, 0, 0)),
            pl.BlockSpec((1, A_WIDTH), const2),
            pl.BlockSpec((256, 256), const2),
            pl.BlockSpec((1, 256), const2),
            pl.BlockSpec((1, d, d), lambda b, j: (layer, 0, 0)),
            pl.BlockSpec((1, d), const2),
            pl.BlockSpec((N_EXPERTS, d), const2),
            pl.BlockSpec((N_EXPERTS, 1), const2),
            pl.BlockSpec((A_WIDTH, A_WIDTH), const2),
        ],
        out_specs=[
            pl.BlockSpec((1, bt, d), tok),
            pl.BlockSpec((MOE_NC, bt, MOE_CW), lambda b, j: (0, b * nt + j, 0)),
            pl.BlockSpec((1, 2, bt), lambda b, j: (b, 0, j)),
            pl.BlockSpec((1, bt, 128), tok),
        ],
        compiler_params=_cparams(("parallel", "arbitrary")),
        name="mix_out_proj_router",
    )(xs, of, ob, hg, pool, pool, pool, oc, mods, mods, ng, pw, ps, wo, g2, rw_t, rb, bda)


def _moe_kernel(be_ref, nb_ref, x_ref, wg_ref, wu_ref, wd_ref, y_ref, wgb, wub, wdb):
    i = pl.program_id(0)
    prev = be_ref[jnp.maximum(i - 1, 0)]
    fresh = jnp.logical_or(i == 0, be_ref[i] != prev)

    @pl.when(fresh)
    def _():
        wgb[...] = wg_ref[0, 0].astype(BF16)
        wub[...] = wu_ref[0, 0].astype(BF16)
        wdb[...] = wd_ref[0, 0].astype(BF16)

    @pl.when(i < nb_ref[0])
    def _():
        x = _unpack_bf16_pairs(_load_planes(x_ref, ())).astype(BF16)
        gate = _dot(x, wgb[...])
        up = _dot(x, wub[...])
        a = gate * jax.nn.sigmoid(gate) * up
        _store_planes(y_ref, (), _pack_bf16_pairs(_dot(a.astype(BF16), wdb[...])))

    @pl.when(i >= nb_ref[0])
    def _():
        y_ref[...] = jnp.zeros_like(y_ref)


def _moe_call(block_expert, nblocks, xg, wg, wu, wd, layer):
    nc, p, cw = xg.shape
    _, _, d, f = wg.shape
    bm = MOE_BM
    grid_spec = pltpu.PrefetchScalarGridSpec(
        num_scalar_prefetch=2,
        grid=(p // bm,),
        in_specs=[
            pl.BlockSpec((nc, bm, cw), lambda i, be, nb: (0, i, 0)),
            pl.BlockSpec((1, 1, d, f), lambda i, be, nb: (layer, be[i], 0, 0)),
            pl.BlockSpec((1, 1, d, f), lambda i, be, nb: (layer, be[i], 0, 0)),
            pl.BlockSpec((1, 1, f, d), lambda i, be, nb: (layer, be[i], 0, 0)),
        ],
        out_specs=pl.BlockSpec((nc, bm, cw), lambda i, be, nb: (0, i, 0)),
        scratch_shapes=[pltpu.VMEM((d, f), BF16), pltpu.VMEM((d, f), BF16), pltpu.VMEM((f, d), BF16)],
    )
    return pl.pallas_call(
        _moe_kernel,
        out_shape=jax.ShapeDtypeStruct((nc, p, cw), xg.dtype),
        grid_spec=grid_spec,
        compiler_params=_cparams(("arbitrary",)),
        name="moe_experts",
    )(block_expert, nblocks, xg, wg, wu, wd)


def _sc_mesh():
    return plsc.VectorSubcoreMesh(core_axis_name="core", subcore_axis_name="subcore")


def _sc_scatter_rows(x, idx, n_out, src_block):
    m = idx.shape[0]

    @pl.kernel(out_type=jax.ShapeDtypeStruct((n_out, x.shape[1]), x.dtype), mesh=_sc_mesh())
    def scatter(x_hbm, i_hbm, o_hbm):
        def body(x_vmem, i_vmem):
            pltpu.sync_copy(x_vmem, o_hbm.at[i_vmem.at[0]])

        pltpu.emit_pipeline(
            body,
            grid=(m // SC_WINDOW,),
            in_specs=[pl.BlockSpec((SC_WINDOW, x.shape[1]), lambda w: (src_block(w), 0)),
                      pl.BlockSpec((1, SC_WINDOW), lambda w: (0, w))],
            out_specs=[],
            core_axis_name=("core", "subcore"),
            dimension_semantics=(pltpu.PARALLEL,),
        )(x_hbm, i_hbm)

    return scatter(x, idx.reshape(1, m))


def _sc_gather_rows(x, idx):
    m = idx.shape[0]

    @pl.kernel(out_type=jax.ShapeDtypeStruct((m, x.shape[1]), x.dtype), mesh=_sc_mesh())
    def gather(x_hbm, i_hbm, o_hbm):
        def body(i_vmem, o_vmem):
            pltpu.sync_copy(x_hbm.at[i_vmem.at[0]], o_vmem)

        pltpu.emit_pipeline(
            body,
            grid=(m // SC_WINDOW,),
            in_specs=[pl.BlockSpec((1, SC_WINDOW), lambda w: (0, w))],
            out_specs=[pl.BlockSpec((SC_WINDOW, x.shape[1]), lambda w: (w, 0))],
            core_axis_name=("core", "subcore"),
            dimension_semantics=(pltpu.PARALLEL,),
        )(i_hbm, o_hbm)

    return gather(x, idx.reshape(1, m))


def _moe_plan(idx, n_tokens):
    bm = MOE_BM
    flat = idx.reshape(-1)
    onehot = (flat[:, None] == jnp.arange(N_EXPERTS)[None, :]).astype(jnp.int32)
    rank = jnp.cumsum(onehot, axis=0) - onehot
    counts = jnp.sum(onehot, axis=0)
    padded = ((counts + bm - 1) // bm) * bm
    seg_end = jnp.cumsum(padded)
    seg_start = seg_end - padded
    slot = jnp.sum(onehot * (seg_start[None, :] + rank), axis=1)
    n_assign = flat.shape[0]
    p = (n_assign // bm + N_EXPERTS) * bm
    nb = p // bm
    blk_start = jnp.arange(nb) * bm
    block_expert = jnp.sum((blk_start[:, None] >= seg_end[None, :]).astype(jnp.int32), axis=1)
    nblocks = (seg_end[-1] // bm).astype(jnp.int32)
    last_e = jnp.max(jnp.where(counts > 0, jnp.arange(N_EXPERTS), 0))
    block_expert = jnp.minimum(block_expert, last_e).astype(jnp.int32)
    return slot, block_expert, nblocks.reshape(1), p


def _final_kernel(x_ref, y_ref, w_ref, mod_ref, o_ref):
    o_ref[0] = x_ref[0] + mod_ref[0, 0, 5:6, :] * _moe_combine(y_ref, w_ref, slice(None))


def _final_call(xs, yc, wcol, mods, ctx_len):
    bsz, s, d = xs.shape
    tm = TOKEN_TILE
    nctx = ctx_len // tm
    t = s - ctx_len
    return pl.pallas_call(
        _final_kernel,
        out_shape=jax.ShapeDtypeStruct((bsz, t, d), F32),
        grid=(bsz, t // tm),
        in_specs=[
            pl.BlockSpec((1, tm, d), lambda b, j: (b, j + nctx, 0)),
            pl.BlockSpec((MOE_NC, 1, 2, tm, MOE_CW), lambda b, j: (0, b, 0, j + nctx, 0)),
            pl.BlockSpec((1, tm, 128), lambda b, j: (b, j + nctx, 0)),
            pl.BlockSpec((1, 1, 8, d), lambda b, j: (0, b, 0, 0)),
        ],
        out_specs=pl.BlockSpec((1, tm, d), lambda b, j: (b, j, 0)),
        compiler_params=_cparams(("parallel", "arbitrary")),
        name="final_residual",
    )(xs, yc, wcol, mods)


def _block_diag_ones(width):
    i = np.arange(width) // HD
    return jnp.asarray((i[:, None] == i[None, :]).astype(np.float32), dtype=BF16)


def _rope_tables(t, ctx_len):
    rows_n = t // GRID_W
    row = jnp.repeat(jnp.arange(rows_n), GRID_W).astype(F32)
    col = jnp.tile(jnp.arange(GRID_W), rows_n).astype(F32)
    n_freq = HD // 4
    inv = ROPE_THETA ** (-jnp.arange(n_freq, dtype=F32) / n_freq)
    ang = jnp.concatenate([row[:, None] * inv, col[:, None] * inv], axis=-1)
    cos, sin = jnp.cos(ang), jnp.sin(ang)
    cos_h = jnp.concatenate([cos, cos], axis=-1)
    sin_h = jnp.concatenate([-sin, sin], axis=-1)
    cos_h = jnp.concatenate([jnp.ones((ctx_len, HD), F32), cos_h], axis=0)
    sin_h = jnp.concatenate([jnp.zeros((ctx_len, HD), F32), sin_h], axis=0)
    return jnp.tile(cos_h, (1, 2)), jnp.tile(sin_h, (1, 2))


def kernel(x, c, ctx, c_ctx, w_mod, b_mod, norm1_g, norm2_g, w_in, w_out, hgrn_lb, hgrn_norm_g,
           pool_w, pool_scale, q_norm_g, k_norm_g, router_w, router_b, moe_w_gate, moe_w_up, moe_w_down):
    bsz, t, d = x.shape
    ctx_len = ctx.shape[1]
    depth = w_mod.shape[0]
    s = ctx_len + t
    assert ctx_len % TOKEN_TILE == 0 and s % (TOKEN_TILE * TOKEN_NSUB) == 0 and t % GRID_W == 0
    assert ctx_len % HGRN_CHUNK == 0 and ctx_len % ATTN_TQ == 0

    perm = np.concatenate([np.arange(0, HD, 2), np.arange(1, HD, 2)])
    na = 5 * A_WIDTH
    q0 = na + 256
    nqk = (C_HEADS + C_KV_HEADS) * HD
    w_qk = w_in[:, :, q0:q0 + nqk].reshape(depth, d, nqk // HD, HD // 2, 2)
    w_qk = jnp.swapaxes(w_qk, -1, -2).reshape(depth, d, nqk)
    w_in_b = jnp.concatenate([w_in[:, :, :q0], w_qk, w_in[:, :, q0 + nqk:]], axis=-1).astype(BF16)
    w_out_b = w_out.astype(BF16)
    gq = jnp.tile(q_norm_g[:, perm], (1, C_HEADS)).reshape(depth, 1, C_HEADS * HD)
    gk = jnp.tile(k_norm_g[:, perm], (1, C_KV_HEADS)).reshape(depth, 1, C_KV_HEADS * HD)
    ng = jnp.tile(hgrn_norm_g, (1, A_HEADS)).reshape(depth, 1, A_WIDTH)
    eye = jnp.eye(B_GROUPS, dtype=F32)
    pw_bd = (eye[None, :, None, :, None] * pool_w[:, :, :, None, :]).reshape(depth, 256, 256).astype(BF16)
    sm = jax.nn.softmax(hgrn_lb.astype(F32), axis=0)
    lower = jnp.cumsum(sm, axis=0) - sm[0:1]
    cos_t, sin_t = _rope_tables(t, ctx_len)
    bdq, bdk, bda = _block_diag_ones(C_HEADS * HD), _block_diag_ones(C_KV_HEADS * HD), _block_diag_ones(A_WIDTH)
    rw_t = router_w.T
    rb = router_b.reshape(N_EXPERTS, 1).astype(F32)

    cond = jnp.zeros((8, d), F32).at[:bsz].set(c).at[bsz].set(c_ctx)
    mods = _mod_call(cond, w_mod, b_mod)
    mods = mods[:, :bsz + 1].reshape(depth, bsz + 1, 6, d)
    mods = jnp.pad(mods, ((0, 0), (0, 0), (0, 2), (0, 0)))

    xs = jnp.concatenate([ctx, x], axis=1)
    moe = None
    n = bsz * s
    wps = s // SC_WINDOW

    def src_block(w):
        j = w % wps
        bk = w // wps
        return (bk // (2 * bsz)) * (n // SC_WINDOW) + ((bk // 2) % bsz) * wps + j

    for l in range(depth):
        outs = _in_call(xs, moe, mods[l - 1:l] if l else None, mods[l:l + 1], norm1_g[l:l + 1], w_in_b, l,
                        cos_t, sin_t, gq[l], gk[l], bdq, bdk, ctx_len)
        if moe is not None:
            xs, outs = outs[0], outs[1:]
        hg, pool, qh, kh, vh, qn2, kn2 = outs
        of, ob = _hgrn_call(hg, lower[l], ctx_len)
        oc = _attention(qh, kh, vh, qn2, kn2, ctx_len)
        xs, h2c, idx, wcol = _out_call(xs, of, ob, hg, pool, oc, mods[l:l + 1], ng[l], pw_bd[l],
                                       pool_scale[l:l + 1], w_out_b, l, norm2_g[l:l + 1], rw_t, rb, bda, ctx_len)
        slot, block_expert, nblocks, p = _moe_plan(idx, n)
        rows = (jnp.arange(MOE_NC, dtype=jnp.int32)[:, None] * p + slot[None, :]).reshape(-1)
        xg = _sc_scatter_rows(h2c.reshape(MOE_NC * n, MOE_CW), rows, MOE_NC * p, src_block)
        ys = _moe_call(block_expert, nblocks, xg.reshape(MOE_NC, p, MOE_CW),
                       moe_w_gate, moe_w_up, moe_w_down, l)
        yc = _sc_gather_rows(ys.reshape(MOE_NC * p, MOE_CW), rows)
        moe = (yc.reshape(MOE_NC, bsz, 2, s, MOE_CW), wcol)
    return _final_call(xs, moe[0], moe[1], mods[depth - 1:depth], ctx_len)
```

```python
import functools
import math

import jax
import jax.numpy as jnp
import numpy as np
from jax import lax
from jax.experimental import pallas as pl
from jax.experimental.pallas import tpu as pltpu
from jax.experimental.pallas import tpu_sc as plsc

F32 = jnp.float32
BF16 = jnp.bfloat16

EPS = 1e-6
LB_FLOOR = 1e-30
NEG_BIG = -1e4
GRID_W = 64
ROPE_THETA = 10000.0
HD = 64
A_HEADS = 4
A_WIDTH = A_HEADS * HD
B_GROUPS = 4
POOL_WINDOWS = (2, 4, 8, 16)
POOL_HALO = 8
C_HEADS = 8
C_KV_HEADS = 2
C_GROUP = C_HEADS // C_KV_HEADS
N_EXPERTS = 16
N_GROUPS = 4
E_PER_GROUP = N_EXPERTS // N_GROUPS

TOKEN_TILE = 256
TOKEN_NSUB = 3
HGRN_CHUNK = 128
ATTN_TQ = 256
ATTN_NQ = 3
ATTN_UNROLL = 33
LOG2E = 1.4426950408889634
BOUND_SLACK = 1.02
BOUND_MAX = 48.0
MOE_BM = 512
MOE_NC = 2
MOE_CW = 256
SC_WINDOW = 128
VMEM_LIMIT = 56 * 1024 * 1024


def _cparams(sem):
    return pltpu.CompilerParams(dimension_semantics=sem, vmem_limit_bytes=VMEM_LIMIT)


def _dot(a, b):
    return jnp.dot(a, b, preferred_element_type=F32)


def _dot_nt(a, b):
    return lax.dot_general(a, b, (((1,), (1,)), ((), ())), preferred_element_type=F32)


def _dot_tn(a, b):
    return lax.dot_general(a, b, (((0,), (0,)), ((), ())), preferred_element_type=F32)


def _split_bf16(x):
    hi = x.astype(BF16)
    lo = (x - hi.astype(F32)).astype(BF16)
    return hi, lo


def _seg_mean_sq(x, bd):
    hi, lo = _split_bf16(x * x)
    return (_dot(hi, bd) + _dot(lo, bd)) * (1.0 / HD)


def _swap_halves(x):
    w = x.shape[-1]
    lane = lax.broadcasted_iota(jnp.int32, x.shape, x.ndim - 1)
    up = pltpu.roll(x, w - HD // 2, axis=x.ndim - 1)
    dn = pltpu.roll(x, HD // 2, axis=x.ndim - 1)
    return jnp.where((lane & (HD // 2)) == 0, up, dn)


def _mod_kernel(c_ref, w_ref, b_ref, o_ref):
    c = c_ref[...]
    a = c * jax.nn.sigmoid(c)
    hi, lo = _split_bf16(a)
    whi, wlo = _split_bf16(w_ref[0])
    o_ref[0] = _dot(hi, whi) + _dot(lo, whi) + _dot(hi, wlo) + b_ref[0]


def _mod_call(cond, w_mod, b_mod):
    depth, d, n6 = w_mod.shape
    tn = 1536
    return pl.pallas_call(
        _mod_kernel,
        out_shape=jax.ShapeDtypeStruct((depth, 8, n6), F32),
        grid=(depth, n6 // tn),
        in_specs=[
            pl.BlockSpec((8, d), lambda l, j: (0, 0)),
            pl.BlockSpec((1, d, tn), lambda l, j: (l, 0, j)),
            pl.BlockSpec((1, 1, tn), lambda l, j: (l, 0, j)),
        ],
        out_specs=pl.BlockSpec((1, 8, tn), lambda l, j: (l, 0, j)),
        compiler_params=_cparams(("arbitrary", "arbitrary")),
        name="adaln_mod",
    )(cond, w_mod, b_mod.reshape(depth, 1, n6))


def _pack_bf16_pairs(x):
    half = x.shape[1] // 2
    bits = lax.bitcast_convert_type(x.astype(BF16).astype(F32), jnp.uint32)
    return (bits[:, :half] >> 16) | (bits[:, half:] & jnp.uint32(0xFFFF0000))


def _unpack_bf16_pairs(words):
    lo = lax.bitcast_convert_type(words << 16, F32)
    hi = lax.bitcast_convert_type(words & jnp.uint32(0xFFFF0000), F32)
    return jnp.concatenate([lo, hi], axis=1)


def _store_planes(ref, lead, words):
    for c in range(MOE_NC):
        ref[(c,) + lead if lead else c] = words[:, c * MOE_CW:(c + 1) * MOE_CW]


def _load_planes(ref, lead):
    return jnp.concatenate([ref[(c,) + lead if lead else c] for c in range(MOE_NC)], axis=1)


def _moe_combine(y_ref, w_ref, rows):
    y0 = _unpack_bf16_pairs(_load_planes(y_ref, (0, 0, rows, slice(None))))
    y1 = _unpack_bf16_pairs(_load_planes(y_ref, (0, 1, rows, slice(None))))
    return w_ref[0, rows, 0:1] * y0 + w_ref[0, rows, 1:2] * y1


def _in_kernel(*refs, has_moe, nsub, nctx):
    if has_moe:
        (x_ref, y_ref, wc_ref, modp_ref, modpc_ref, mod_ref, modc_ref, g_ref, w_ref, cos_ref, sin_ref, gq_ref,
         gk_ref, bdq_ref, bdk_ref, selq_ref, xo_ref, hg_ref, pool_ref, q_ref, k_ref, v_ref, qn2_ref,
         kn2_ref) = refs
    else:
        (x_ref, mod_ref, modc_ref, g_ref, w_ref, cos_ref, sin_ref, gq_ref, gk_ref,
         bdq_ref, bdk_ref, selq_ref, hg_ref, pool_ref, q_ref, k_ref, v_ref, qn2_ref, kn2_ref) = refs
    tm = TOKEN_TILE
    na = 5 * A_WIDTH
    q0 = na + 256
    qw = C_HEADS * HD
    kw = C_KV_HEADS * HD
    for r in range(nsub):
        rows = slice(r * tm, (r + 1) * tm)
        is_ctx = pl.program_id(1) * nsub + r < nctx
        mod = jnp.where(is_ctx, modc_ref[0, 0], mod_ref[0, 0])
        x = x_ref[0, rows, :]
        if has_moe:
            ga2 = jnp.where(is_ctx, modpc_ref[0, 0, 5:6, :], modp_ref[0, 0, 5:6, :])
            x = x + ga2 * _moe_combine(y_ref, wc_ref, rows)
            xo_ref[0, rows, :] = x
        ms = jnp.mean(x * x, axis=-1, keepdims=True)
        h = x * lax.rsqrt(ms + EPS) * g_ref[...]
        h = h * (1.0 + mod[1:2, :]) + mod[0:1, :]
        y = _dot(h.astype(BF16), w_ref[0])
        hg_ref[0, rows, :] = y[:, :na]
        pool_ref[0, rows, :] = y[:, na:na + 256]
        cos = cos_ref[rows, :]
        sin = sin_ref[rows, :]
        q = y[:, q0:q0 + qw]
        qn = q * lax.rsqrt(_seg_mean_sq(q, bdq_ref[...]) + EPS) * gq_ref[...]
        cos_q = jnp.concatenate([cos] * (qw // 128), axis=1)
        sin_q = jnp.concatenate([sin] * (qw // 128), axis=1)
        qr = (qn * cos_q + _swap_halves(qn) * sin_q) * (HD ** -0.5 * LOG2E)
        for hh in range(C_HEADS):
            q_ref[0, hh, rows, :] = qr[:, hh * HD:(hh + 1) * HD].astype(BF16)
        k = y[:, q0 + qw:q0 + qw + kw]
        kn = k * lax.rsqrt(_seg_mean_sq(k, bdk_ref[...]) + EPS) * gk_ref[...]
        kr = kn * cos + _swap_halves(kn) * sin
        v = y[:, q0 + qw + kw:q0 + qw + 2 * kw]
        vt = v.T.astype(BF16)
        for hh in range(C_KV_HEADS):
            k_ref[0, hh, rows, :] = kr[:, hh * HD:(hh + 1) * HD].astype(BF16)
            v_ref[0, hh, r] = vt[hh * HD:(hh + 1) * HD, :]
        qn2 = _dot_nt(selq_ref[...], (qr * qr).astype(BF16))
        for hh in range(C_KV_HEADS):
            qn2_ref[0, hh, r] = qn2[hh * C_GROUP:(hh + 1) * C_GROUP, :]
        kn2 = _dot((kr * kr).astype(BF16), bdk_ref[...])
        kn2_ref[0, r] = jnp.max(kn2, axis=0, keepdims=True)


def _in_call(xs, moe, mods_prev, mods, g1, w_in, layer, cos_t, sin_t, gq, gk, bdq, bdk, ctx_len):
    bsz, s, d = xs.shape
    tm = TOKEN_TILE
    nsub = TOKEN_NSUB
    bt = nsub * tm
    ncols = w_in.shape[2]
    has_moe = moe is not None

    def tok(b, j):
        return (b, j, 0)

    batch_mod = pl.BlockSpec((1, 1, 8, d), lambda b, j: (0, b, 0, 0))
    ctx_mod = pl.BlockSpec((1, 1, 8, d), lambda b, j: (0, bsz, 0, 0))
    const2 = lambda b, j: (0, 0)
    in_specs = [pl.BlockSpec((1, bt, d), tok)]
    args = [xs]
    if has_moe:
        in_specs += [pl.BlockSpec((MOE_NC, 1, 2, bt, MOE_CW), lambda b, j: (0, b, 0, j, 0)),
                     pl.BlockSpec((1, bt, 128), tok), batch_mod, ctx_mod]
        args += [moe[0], moe[1], mods_prev, mods_prev]
    in_specs += [
        batch_mod,
        ctx_mod,
        pl.BlockSpec((1, d), const2),
        pl.BlockSpec((1, d, ncols), lambda b, j: (layer, 0, 0)),
        pl.BlockSpec((bt, 128), lambda b, j: (j, 0)),
        pl.BlockSpec((bt, 128), lambda b, j: (j, 0)),
        pl.BlockSpec((1, C_HEADS * HD), const2),
        pl.BlockSpec((1, C_KV_HEADS * HD), const2),
        pl.BlockSpec((C_HEADS * HD, C_HEADS * HD), const2),
        pl.BlockSpec((C_KV_HEADS * HD, C_KV_HEADS * HD), const2),
        pl.BlockSpec((C_HEADS, C_HEADS * HD), const2),
    ]
    selq = bdq[::HD]
    args += [mods, mods, g1, w_in, cos_t, sin_t, gq, gk, bdq, bdk, selq]
    out_shape = []
    out_specs = []
    if has_moe:
        out_shape.append(jax.ShapeDtypeStruct((bsz, s, d), F32))
        out_specs.append(pl.BlockSpec((1, bt, d), tok))
    out_shape += [
        jax.ShapeDtypeStruct((bsz, s, 5 * A_WIDTH), F32),
        jax.ShapeDtypeStruct((bsz, s, 256), F32),
        jax.ShapeDtypeStruct((bsz, C_HEADS, s, HD), BF16),
        jax.ShapeDtypeStruct((bsz, C_KV_HEADS, s, HD), BF16),
        jax.ShapeDtypeStruct((bsz, C_KV_HEADS, s // tm, HD, tm), BF16),
        jax.ShapeDtypeStruct((bsz, C_KV_HEADS, s // tm, C_GROUP, tm), F32),
        jax.ShapeDtypeStruct((bsz, s // tm, 1, C_KV_HEADS * HD), F32),
    ]
    out_specs += [
        pl.BlockSpec((1, bt, 5 * A_WIDTH), tok),
        pl.BlockSpec((1, bt, 256), tok),
        pl.BlockSpec((1, C_HEADS, bt, HD), lambda b, j: (b, 0, j, 0)),
        pl.BlockSpec((1, C_KV_HEADS, bt, HD), lambda b, j: (b, 0, j, 0)),
        pl.BlockSpec((1, C_KV_HEADS, nsub, HD, tm), lambda b, j: (b, 0, j, 0, 0)),
        pl.BlockSpec((1, C_KV_HEADS, nsub, C_GROUP, tm), lambda b, j: (b, 0, j, 0, 0)),
        pl.BlockSpec((1, nsub, 1, C_KV_HEADS * HD), lambda b, j: (b, j, 0, 0)),
    ]
    return pl.pallas_call(
        functools.partial(_in_kernel, has_moe=has_moe, nsub=nsub, nctx=ctx_len // tm),
        out_shape=out_shape,
        grid=(bsz, s // bt),
        in_specs=in_specs,
        out_specs=out_specs,
        compiler_params=_cparams(("parallel", "arbitrary")),
        name="norm_in_proj",
    )(*args)


def _bcast_rows(x, period, off):
    n, w = x.shape
    if period >= 8:
        g = x.reshape(n // period, period, w)
        return jnp.broadcast_to(g[:, off:off + 1, :], g.shape).reshape(n, w)
    g = x.reshape(n // 8, 8, w)
    sub = lax.broadcasted_iota(jnp.int32, g.shape, 1)
    out = jnp.broadcast_to(g[:, off:off + 1, :], g.shape)
    for i in range(1, 8 // period):
        piece = jnp.broadcast_to(g[:, i * period + off:i * period + off + 1, :], g.shape)
        out = jnp.where(sub >= i * period, piece, out)
    return out.reshape(n, w)


def _hgrn_pair_masks(chunk):
    idx = np.arange(chunk)
    out = np.zeros((2, 1 + int(math.log2(chunk)), chunk, chunk), np.float32)
    for d, tau in enumerate((idx, chunk - 1 - idx)):
        out[d, 0] = np.eye(chunk)
        for level in range(int(math.log2(chunk))):
            c = 1 << level
            later = (tau & c) != 0
            same = (tau[:, None] >> (level + 1)) == (tau[None, :] >> (level + 1))
            out[d, 1 + level] = same & later[:, None] & ~later[None, :]
    return jnp.asarray(out)


def _hgrn_chunk(qz, fz, vz, lb, mask_ref, hm_ref, bd_ref, st_ref, rev, chunk):
    w = A_WIDTH
    row = lax.broadcasted_iota(jnp.int32, (chunk, w), 0)
    tau = (chunk - 1 - row) if rev else row

    q = qz * jax.nn.sigmoid(qz)
    lbf = jnp.maximum(lb, LB_FLOOR)
    e = jnp.exp(-jnp.abs(fz))
    r = 1.0 / (1.0 + e)
    er = e * r
    pos = fz >= 0
    f = lbf + (1.0 - lbf) * jnp.where(pos, r, er)
    kk = (1.0 - lbf) * jnp.where(pos, er, r)
    b = jnp.log2(f)
    step = 1
    while step < chunk:
        b = b + jnp.where(tau >= step, pltpu.roll(b, (chunk - step) if rev else step, axis=0), 0.0)
        step *= 2

    def head_stack(x16):
        return jnp.concatenate([x16 * hm_ref[hh] for hh in range(A_HEADS)], axis=0)

    def masked_pairs(lhs16, rhs16, m):
        p = _dot_nt(lhs16, head_stack(rhs16))
        return jnp.concatenate([p[:, hh * chunk:(hh + 1) * chunk] * m for hh in range(A_HEADS)], axis=1)

    att = masked_pairs(q.astype(BF16), kk.astype(BF16), mask_ref[0])
    for level in range(int(math.log2(chunk))):
        c = 1 << level
        later = (tau & c) != 0
        if level == 0:
            z = jnp.where(later, q * f, kk)
        else:
            mid = _bcast_rows(b, 2 * c, c if rev else c - 1)
            z = jnp.where(later, q, kk) * jnp.exp2(-jnp.abs(b - mid))
        z = z.astype(BF16)
        att = att + masked_pairs(z, z, mask_ref[1 + level])

    b_end = b[0:1, :] if rev else b[chunk - 1:chunk, :]
    qin = (q * jnp.exp2(b)).astype(BF16)
    kend = (kk * jnp.exp2(b_end - b)).astype(BF16)
    vb = vz.astype(BF16)
    st = st_ref[...]
    o = _dot_nt(qin, st.astype(BF16)) + _dot(att.astype(BF16), head_stack(vb))
    st_ref[...] = (st * jnp.exp2(b_end) + _dot_tn(vb, kend)) * bd_ref[...]
    return o


def _hgrn_kernel(qf_ref, ff_ref, vf_ref, qb_ref, fb_ref, vb_ref, lb_ref, mask_ref, hm_ref, bd_ref,
                 of_ref, ob_ref, st_ref, *, chunk):
    @pl.when(pl.program_id(1) == 0)
    def _():
        st_ref[...] = jnp.zeros_like(st_ref)

    of_ref[0] = _hgrn_chunk(qf_ref[0], ff_ref[0], vf_ref[0], lb_ref[0:1, :], mask_ref.at[0], hm_ref, bd_ref,
                            st_ref.at[0], False, chunk)
    ob_ref[0] = _hgrn_chunk(qb_ref[0], fb_ref[0], vb_ref[0], lb_ref[1:2, :], mask_ref.at[1], hm_ref, bd_ref,
                            st_ref.at[1], True, chunk)


def _hgrn_call(hg, lower, ctx_len):
    bsz, s, _ = hg.shape
    chunk = HGRN_CHUNK
    n = s // chunk
    nctx = ctx_len // chunk
    w = A_WIDTH
    masks = _hgrn_pair_masks(chunk)
    head_of_lane = np.arange(w) // HD
    head_masks = jnp.asarray(head_of_lane[None, None, :] == np.arange(A_HEADS)[:, None, None], dtype=BF16)
    block_diag = jnp.asarray(head_of_lane[:, None] == head_of_lane[None, :], dtype=F32)

    def rmap(i):
        return jnp.where(i < nctx, nctx - 1 - i, n + nctx - 1 - i)

    def fwd(col):
        return pl.BlockSpec((1, chunk, w), lambda b, i: (b, i, col))

    def bwd(col):
        return pl.BlockSpec((1, chunk, w), lambda b, i: (b, rmap(i), col))

    return pl.pallas_call(
        functools.partial(_hgrn_kernel, chunk=chunk),
        out_shape=[jax.ShapeDtypeStruct((bsz, s, w), F32)] * 2,
        grid=(bsz, n),
        in_specs=[fwd(0), fwd(1), fwd(3), bwd(0), bwd(2), bwd(3),
                  pl.BlockSpec((2, w), lambda b, i: (0, 0)),
                  pl.BlockSpec(masks.shape, lambda b, i: (0, 0, 0, 0)),
                  pl.BlockSpec((A_HEADS, 1, w), lambda b, i: (0, 0, 0)),
                  pl.BlockSpec((w, w), lambda b, i: (0, 0))],
        out_specs=[pl.BlockSpec((1, chunk, w), lambda b, i: (b, i, 0)),
                   pl.BlockSpec((1, chunk, w), lambda b, i: (b, rmap(i), 0))],
        scratch_shapes=[pltpu.VMEM((2, w, w), F32)],
        compiler_params=_cparams(("parallel", "arbitrary")),
        name="hgrn2_scan",
    )(hg, hg, hg, hg, hg, hg, lower, masks, head_masks, block_diag)


def _attn_kernel(kmax_ref, q_ref, k_ref, vt_ref, qn2_ref, o_ref, *, tq, nq, ts, n_sub, n_ctx_sub, nctx_q,
                 unroll, bounded):
    w = C_GROUP * tq

    def keys(c0, nc):
        return k_ref[0, 0, pl.ds(pl.multiple_of(c0 * ts, ts), nc * ts), :]

    def values_dot(c0, nc, p):
        pb = p.astype(BF16)
        out = _dot(vt_ref[0, 0, c0], pb[0:ts])
        for c in range(1, nc):
            out = out + _dot(vt_ref[0, 0, c0 + c], pb[c * ts:(c + 1) * ts])
        return out

    def tile(u):
        q = q_ref[0, :, pl.ds(pl.multiple_of(u * tq, tq), tq), :].reshape(w, HD)
        if bounded:
            kmax = kmax_ref[pl.program_id(0) * C_KV_HEADS + pl.program_id(1)]
            qn2 = jnp.concatenate([qn2_ref[0, 0, u, g:g + 1, :] for g in range(C_GROUP)], axis=1)
            shift = jnp.sqrt(qn2) * kmax

            def sub_tile(c0, nc, carry):
                l8, acc = carry
                p = jnp.exp2(_dot_nt(keys(c0, nc), q) - shift)
                l8 = l8 + jnp.sum(p.reshape(nc * ts // 8, 8, w), axis=0)
                return l8, acc + values_dot(c0, nc, p)

            init = (jnp.zeros((8, w), F32), jnp.zeros((HD, w), F32))
        else:
            def sub_tile(c0, nc, carry):
                m_prev, l_prev, acc = carry
                s = _dot_nt(keys(c0, nc), q)
                m_new = jnp.maximum(m_prev, jnp.max(s, axis=0, keepdims=True))
                alpha = jnp.exp2(m_prev - m_new)
                p = jnp.exp2(s - m_new)
                l_new = alpha * l_prev + jnp.sum(p, axis=0, keepdims=True)
                return m_new, l_new, alpha * acc + values_dot(c0, nc, p)

            init = (jnp.full((1, w), -jnp.inf, F32), jnp.zeros((1, w), F32), jnp.zeros((HD, w), F32))

        def run(n):
            carry = init
            if n >= unroll:
                carry = lax.fori_loop(0, n // unroll, lambda i, c: sub_tile(i * unroll, unroll, c), carry)
            if n % unroll:
                carry = sub_tile((n // unroll) * unroll, n % unroll, carry)
            l_fin = jnp.sum(carry[-2], axis=0, keepdims=True)
            ot = carry[-1] / l_fin
            o_ref[0, 0, u] = jnp.concatenate([ot[:, g * tq:(g + 1) * tq] for g in range(C_GROUP)],
                                             axis=0).astype(o_ref.dtype)

        is_ctx = pl.program_id(2) * nq + u < nctx_q

        @pl.when(is_ctx)
        def _():
            run(n_ctx_sub)

        @pl.when(jnp.logical_not(is_ctx))
        def _():
            run(n_sub)

    def tile_step(u, carry):
        tile(u)
        return carry

    lax.fori_loop(0, nq, tile_step, 0)


def _attn_call(kmax, q, k, vt, qn2, ctx_len, bounded):
    bsz, _, s, _ = q.shape
    tq = ATTN_TQ
    nq = ATTN_NQ
    ts = vt.shape[-1]
    assert ctx_len % ts == 0 and s % (nq * tq) == 0 and qn2.shape[-1] == tq
    n_sub = s // ts
    grid_spec = pltpu.PrefetchScalarGridSpec(
        num_scalar_prefetch=1,
        grid=(bsz, C_KV_HEADS, s // (nq * tq)),
        in_specs=[
            pl.BlockSpec((1, C_GROUP, nq * tq, HD), lambda b, g, qi, km: (b, g, qi, 0)),
            pl.BlockSpec((1, 1, s, HD), lambda b, g, qi, km: (b, g, 0, 0)),
            pl.BlockSpec((1, 1, n_sub, HD, ts), lambda b, g, qi, km: (b, g, 0, 0, 0)),
            pl.BlockSpec((1, 1, nq, C_GROUP, tq), lambda b, g, qi, km: (b, g, qi, 0, 0)),
        ],
        out_specs=pl.BlockSpec((1, 1, nq, C_GROUP * HD, tq), lambda b, g, qi, km: (b, g, qi, 0, 0)),
    )
    return pl.pallas_call(
        functools.partial(_attn_kernel, tq=tq, nq=nq, ts=ts, n_sub=n_sub, n_ctx_sub=ctx_len // ts,
                          nctx_q=ctx_len // tq, unroll=ATTN_UNROLL, bounded=bounded),
        out_shape=jax.ShapeDtypeStruct((bsz, C_KV_HEADS, s // tq, C_GROUP * HD, tq), BF16),
        grid_spec=grid_spec,
        compiler_params=_cparams(("parallel", "parallel", "arbitrary")),
        name="gqa_attention_bounded" if bounded else "gqa_attention",
    )(kmax, q, k, vt, qn2)


def _attention(q, k, vt, qn2, kn2, ctx_len):
    bsz = q.shape[0]
    k2 = jnp.max(kn2, axis=(1, 2)).reshape(bsz, C_KV_HEADS, HD)[:, :, 0]
    q2 = jnp.max(qn2.reshape(bsz, C_KV_HEADS, -1), axis=-1)
    kmax = (jnp.sqrt(k2) * BOUND_SLACK).reshape(-1)
    small = jnp.max(jnp.sqrt(q2).reshape(-1) * kmax) <= BOUND_MAX
    return lax.cond(small,
                    functools.partial(_attn_call, ctx_len=ctx_len, bounded=True),
                    functools.partial(_attn_call, ctx_len=ctx_len, bounded=False),
                    kmax, q, k, vt, qn2)


def _route(logits_t, bias):
    scores = jax.nn.sigmoid(logits_t)
    sel = scores + bias
    rows = [sel[e:e + 1, :] for e in range(N_EXPERTS)]
    srow = [scores[e:e + 1, :] for e in range(N_EXPERTS)]
    best_val = None
    best_grp = None
    for g in range(N_GROUPS):
        a = rows[g * E_PER_GROUP:(g + 1) * E_PER_GROUP]
        m1 = a[0]
        i1 = jnp.zeros_like(a[0], dtype=jnp.int32)
        for j in range(1, E_PER_GROUP):
            take = a[j] > m1
            m1 = jnp.where(take, a[j], m1)
            i1 = jnp.where(take, j, i1)
        m2 = jnp.full_like(m1, -jnp.inf)
        for j in range(E_PER_GROUP):
            m2 = jnp.where(i1 == j, m2, jnp.maximum(m2, a[j]))
        gs = m1 + m2
        if g == 0:
            best_val, best_grp = gs, jnp.zeros_like(i1)
        else:
            take = gs > best_val
            best_val = jnp.where(take, gs, best_val)
            best_grp = jnp.where(take, g, best_grp)
    masked = [jnp.where(best_grp == (e // E_PER_GROUP), rows[e], NEG_BIG) for e in range(N_EXPERTS)]
    v1 = masked[0]
    e1 = jnp.zeros_like(best_grp)
    for e in range(1, N_EXPERTS):
        take = masked[e] > v1
        v1 = jnp.where(take, masked[e], v1)
        e1 = jnp.where(take, e, e1)
    v2 = jnp.full_like(v1, -jnp.inf)
    e2 = jnp.zeros_like(best_grp)
    for e in range(N_EXPERTS):
        take = jnp.logical_and(e1 != e, masked[e] > v2)
        v2 = jnp.where(take, masked[e], v2)
        e2 = jnp.where(take, e, e2)
    w1 = jnp.zeros_like(v1)
    w2 = jnp.zeros_like(v1)
    for e in range(N_EXPERTS):
        w1 = jnp.where(e1 == e, srow[e], w1)
        w2 = jnp.where(e2 == e, srow[e], w2)
    tot = w1 + w2
    return jnp.concatenate([e1, e2], axis=0), jnp.concatenate([w1 / tot, w2 / tot], axis=0)


def _out_kernel(x_ref, of_ref, ob_ref, g_ref, pc_ref, pp_ref, pn_ref, oc_ref, mod_ref, modc_ref, ng_ref, pw_ref,
                ps_ref, wo_ref, g2_ref, rw_ref, rb_ref, bda_ref,
                xo_ref, h2_ref, idx_ref, wgt_ref, *, nsub, seg_tiles):
    tm = TOKEN_TILE
    gw = C_GROUP * HD
    nctx, ntot = seg_tiles
    n = tm + 2 * POOL_HALO
    lane = lax.broadcasted_iota(jnp.int32, (tm, 256), 1)
    rhi, rlo = _split_bf16(rw_ref[...])
    for r in range(nsub):
        rows = slice(r * tm, (r + 1) * tm)
        j = pl.program_id(1) * nsub + r
        mod = jnp.where(j < nctx, modc_ref[0, 0], mod_ref[0, 0])
        o = of_ref[0, rows, :] + ob_ref[0, rows, :]
        gz = g_ref[0, rows, :]
        oa = o * lax.rsqrt(_seg_mean_sq(o, bda_ref[...]) + EPS) * ng_ref[...] * (gz * jax.nn.sigmoid(gz))
        first = jnp.logical_or(j == 0, j == nctx)
        last = jnp.logical_or(j == nctx - 1, j == ntot - 1)
        cur = pc_ref[0, rows, :]
        before = pp_ref[0] if r == 0 else pc_ref[0, r * tm - POOL_HALO:r * tm, :]
        after = pn_ref[0] if r == nsub - 1 else pc_ref[0, (r + 1) * tm:(r + 1) * tm + POOL_HALO, :]
        ext = jnp.concatenate([jnp.where(first, 0.0, before), cur, jnp.where(last, 0.0, after)], axis=0)
        s2 = ext + pltpu.roll(ext, 1, axis=0)
        s4 = pltpu.roll(s2, 1, axis=0) + pltpu.roll(s2, n - 1, axis=0)
        s8 = pltpu.roll(s4, 2, axis=0) + pltpu.roll(s4, n - 2, axis=0)
        s16 = pltpu.roll(s8, 4, axis=0) + pltpu.roll(s8, n - 4, axis=0)
        sums = [a[POOL_HALO:POOL_HALO + tm, :] for a in (s2, s4, s8, s16)]
        seg_start = jnp.where(j < nctx, 0, nctx) * tm
        seg_len = jnp.where(j < nctx, nctx, ntot - nctx) * tm
        t = j * tm - seg_start + lax.broadcasted_iota(jnp.int32, (tm, 1), 0)
        mean = jnp.zeros((tm, 256), F32)
        for gi, win in enumerate(POOL_WINDOWS):
            cnt = (jnp.minimum(t + win // 2, seg_len) - jnp.maximum(t - win // 2, 0)).astype(F32)
            mean = jnp.where((lane >> 6) == gi, sums[gi] / cnt, mean)
        ob = _dot((mean - cur).astype(BF16), pw_ref[...]) * ps_ref[...]
        mix = (_dot(oa.astype(BF16), wo_ref[0, 0:256, :]) + _dot(ob.astype(BF16), wo_ref[0, 256:512, :])
               + sum(_dot_tn(oc_ref[0, g, r], wo_ref[0, 512 + g * gw:512 + (g + 1) * gw, :])
                     for g in range(C_KV_HEADS)))
        x = x_ref[0, rows, :] + mod[2:3, :] * mix
        xo_ref[0, rows, :] = x
        ms = jnp.mean(x * x, axis=-1, keepdims=True)
        h2 = x * lax.rsqrt(ms + EPS) * g2_ref[...]
        h2 = h2 * (1.0 + mod[4:5, :]) + mod[3:4, :]
        _store_planes(h2_ref, (rows, slice(None)), _pack_bf16_pairs(h2))
        hhi, hlo = _split_bf16(h2)
        logits_t = _dot_nt(rhi, hhi) + _dot_nt(rhi, hlo) + _dot_nt(rlo, hhi)
        ids, wts = _route(logits_t, rb_ref[...])
        idx_ref[0, :, rows] = ids
        wgt_ref[0, rows, :] = jnp.concatenate([wts, jnp.zeros((128 - 2, tm), F32)], axis=0).T


def _out_call(xs, of, ob, hg, pool, oc, mods, ng, pw, ps, wo, layer, g2, rw_t, rb, bda, ctx_len):
    bsz, s, d = xs.shape
    tm = TOKEN_TILE
    nsub = TOKEN_NSUB
    bt = nsub * tm
    nt = s // bt
    hb = bt // POOL_HALO
    nhb = s // POOL_HALO

    def tok(b, j):
        return (b, j, 0)

    const2 = lambda b, j: (0, 0)
    return pl.pallas_call(
        functools.partial(_out_kernel, nsub=nsub, seg_tiles=(ctx_len // tm, s // tm)),
        out_shape=[
            jax.ShapeDtypeStruct((bsz, s, d), F32),
            jax.ShapeDtypeStruct((MOE_NC, bsz * s, MOE_CW), jnp.uint32),
            jax.ShapeDtypeStruct((bsz, 2, s), jnp.int32),
            jax.ShapeDtypeStruct((bsz, s, 128), F32),
        ],
        grid=(bsz, nt),
        in_specs=[
            pl.BlockSpec((1, bt, d), tok),
            pl.BlockSpec((1, bt, A_WIDTH), tok),
            pl.BlockSpec((1, bt, A_WIDTH), tok),
            pl.BlockSpec((1, bt, A_WIDTH), lambda b, j: (b, j, 4)),
            pl.BlockSpec((1, bt, 256), tok),
            pl.BlockSpec((1, POOL_HALO, 256), lambda b, j: (b, jnp.maximum(j * hb - 1, 0), 0)),
            pl.BlockSpec((1, POOL_HALO, 256), lambda b, j: (b, jnp.minimum((j + 1) * hb, nhb - 1), 0)),
            pl.BlockSpec((1, C_KV_HEADS, nsub, C_GROUP * HD, tm), lambda b, j: (b, 0, j, 0, 0)),
            pl.BlockSpec((1, 1, 8, d), lambda b, j: (0, b, 0, 0)),
            pl.BlockSpec((1, 1, 8, d), lambda b, j: (0, bsz, 0, 0)),
            pl.BlockSpec((1, A_WIDTH), const2),
            pl.BlockSpec((256, 256), const2),
            pl.BlockSpec((1, 256), const2),
            pl.BlockSpec((1, d, d), lambda b, j: (layer, 0, 0)),
            pl.BlockSpec((1, d), const2),
            pl.BlockSpec((N_EXPERTS, d), const2),
            pl.BlockSpec((N_EXPERTS, 1), const2),
            pl.BlockSpec((A_WIDTH, A_WIDTH), const2),
        ],
        out_specs=[
            pl.BlockSpec((1, bt, d), tok),
            pl.BlockSpec((MOE_NC, bt, MOE_CW), lambda b, j: (0, b * nt + j, 0)),
            pl.BlockSpec((1, 2, bt), lambda b, j: (b, 0, j)),
            pl.BlockSpec((1, bt, 128), tok),
        ],
        compiler_params=_cparams(("parallel", "arbitrary")),
        name="mix_out_proj_router",
    )(xs, of, ob, hg, pool, pool, pool, oc, mods, mods, ng, pw, ps, wo, g2, rw_t, rb, bda)


def _moe_kernel(be_ref, nb_ref, x_ref, wg_ref, wu_ref, wd_ref, y_ref, wgb, wub, wdb):
    i = pl.program_id(0)
    prev = be_ref[jnp.maximum(i - 1, 0)]
    fresh = jnp.logical_or(i == 0, be_ref[i] != prev)

    @pl.when(fresh)
    def _():
        wgb[...] = wg_ref[0, 0].astype(BF16)
        wub[...] = wu_ref[0, 0].astype(BF16)
        wdb[...] = wd_ref[0, 0].astype(BF16)

    @pl.when(i < nb_ref[0])
    def _():
        x = _unpack_bf16_pairs(_load_planes(x_ref, ())).astype(BF16)
        gate = _dot(x, wgb[...])
        up = _dot(x, wub[...])
        a = gate * jax.nn.sigmoid(gate) * up
        _store_planes(y_ref, (), _pack_bf16_pairs(_dot(a.astype(BF16), wdb[...])))

    @pl.when(i >= nb_ref[0])
    def _():
        y_ref[...] = jnp.zeros_like(y_ref)


def _moe_call(block_expert, nblocks, xg, wg, wu, wd, layer):
    nc, p, cw = xg.shape
    _, _, d, f = wg.shape
    bm = MOE_BM
    grid_spec = pltpu.PrefetchScalarGridSpec(
        num_scalar_prefetch=2,
        grid=(p // bm,),
        in_specs=[
            pl.BlockSpec((nc, bm, cw), lambda i, be, nb: (0, i, 0)),
            pl.BlockSpec((1, 1, d, f), lambda i, be, nb: (layer, be[i], 0, 0)),
            pl.BlockSpec((1, 1, d, f), lambda i, be, nb: (layer, be[i], 0, 0)),
            pl.BlockSpec((1, 1, f, d), lambda i, be, nb: (layer, be[i], 0, 0)),
        ],
        out_specs=pl.BlockSpec((nc, bm, cw), lambda i, be, nb: (0, i, 0)),
        scratch_shapes=[pltpu.VMEM((d, f), BF16), pltpu.VMEM((d, f), BF16), pltpu.VMEM((f, d), BF16)],
    )
    return pl.pallas_call(
        _moe_kernel,
        out_shape=jax.ShapeDtypeStruct((nc, p, cw), xg.dtype),
        grid_spec=grid_spec,
        compiler_params=_cparams(("arbitrary",)),
        name="moe_experts",
    )(block_expert, nblocks, xg, wg, wu, wd)


def _sc_mesh():
    return plsc.VectorSubcoreMesh(core_axis_name="core", subcore_axis_name="subcore")


def _sc_scatter_rows(x, idx, n_out, src_block):
    m = idx.shape[0]

    @pl.kernel(out_type=jax.ShapeDtypeStruct((n_out, x.shape[1]), x.dtype), mesh=_sc_mesh())
    def scatter(x_hbm, i_hbm, o_hbm):
        def body(x_vmem, i_vmem):
            pltpu.sync_copy(x_vmem, o_hbm.at[i_vmem.at[0]])

        pltpu.emit_pipeline(
            body,
            grid=(m // SC_WINDOW,),
            in_specs=[pl.BlockSpec((SC_WINDOW, x.shape[1]), lambda w: (src_block(w), 0)),
                      pl.BlockSpec((1, SC_WINDOW), lambda w: (0, w))],
            out_specs=[],
            core_axis_name=("core", "subcore"),
            dimension_semantics=(pltpu.PARALLEL,),
        )(x_hbm, i_hbm)

    return scatter(x, idx.reshape(1, m))


def _sc_gather_rows(x, idx):
    m = idx.shape[0]

    @pl.kernel(out_type=jax.ShapeDtypeStruct((m, x.shape[1]), x.dtype), mesh=_sc_mesh())
    def gather(x_hbm, i_hbm, o_hbm):
        def body(i_vmem, o_vmem):
            pltpu.sync_copy(x_hbm.at[i_vmem.at[0]], o_vmem)

        pltpu.emit_pipeline(
            body,
            grid=(m // SC_WINDOW,),
            in_specs=[pl.BlockSpec((1, SC_WINDOW), lambda w: (0, w))],
            out_specs=[pl.BlockSpec((SC_WINDOW, x.shape[1]), lambda w: (w, 0))],
            core_axis_name=("core", "subcore"),
            dimension_semantics=(pltpu.PARALLEL,),
        )(i_hbm, o_hbm)

    return gather(x, idx.reshape(1, m))


def _moe_plan(idx, n_tokens):
    bm = MOE_BM
    flat = idx.reshape(-1)
    onehot = (flat[:, None] == jnp.arange(N_EXPERTS)[None, :]).astype(jnp.int32)
    rank = jnp.cumsum(onehot, axis=0) - onehot
    counts = jnp.sum(onehot, axis=0)
    padded = ((counts + bm - 1) // bm) * bm
    seg_end = jnp.cumsum(padded)
    seg_start = seg_end - padded
    slot = jnp.sum(onehot * (seg_start[None, :] + rank), axis=1)
    n_assign = flat.shape[0]
    p = (n_assign // bm + N_EXPERTS) * bm
    nb = p // bm
    blk_start = jnp.arange(nb) * bm
    block_expert = jnp.sum((blk_start[:, None] >= seg_end[None, :]).astype(jnp.int32), axis=1)
    nblocks = (seg_end[-1] // bm).astype(jnp.int32)
    last_e = jnp.max(jnp.where(counts > 0, jnp.arange(N_EXPERTS), 0))
    block_expert = jnp.minimum(block_expert, last_e).astype(jnp.int32)
    return slot, block_expert, nblocks.reshape(1), p


def _final_kernel(x_ref, y_ref, w_ref, mod_ref, o_ref):
    o_ref[0] = x_ref[0] + mod_ref[0, 0, 5:6, :] * _moe_combine(y_ref, w_ref, slice(None))


def _final_call(xs, yc, wcol, mods, ctx_len):
    bsz, s, d = xs.shape
    tm = TOKEN_TILE
    nctx = ctx_len // tm
    t = s - ctx_len
    return pl.pallas_call(
        _final_kernel,
        out_shape=jax.ShapeDtypeStruct((bsz, t, d), F32),
        grid=(bsz, t // tm),
        in_specs=[
            pl.BlockSpec((1, tm, d), lambda b, j: (b, j + nctx, 0)),
            pl.BlockSpec((MOE_NC, 1, 2, tm, MOE_CW), lambda b, j: (0, b, 0, j + nctx, 0)),
            pl.BlockSpec((1, tm, 128), lambda b, j: (b, j + nctx, 0)),
            pl.BlockSpec((1, 1, 8, d), lambda b, j: (0, b, 0, 0)),
        ],
        out_specs=pl.BlockSpec((1, tm, d), lambda b, j: (b, j, 0)),
        compiler_params=_cparams(("parallel", "arbitrary")),
        name="final_residual",
    )(xs, yc, wcol, mods)


def _block_diag_ones(width):
    i = np.arange(width) // HD
    return jnp.asarray((i[:, None] == i[None, :]).astype(np.float32), dtype=BF16)


def _rope_tables(t, ctx_len):
    rows_n = t // GRID_W
    row = jnp.repeat(jnp.arange(rows_n), GRID_W).astype(F32)
    col = jnp.tile(jnp.arange(GRID_W), rows_n).astype(F32)
    n_freq = HD // 4
    inv = ROPE_THETA ** (-jnp.arange(n_freq, dtype=F32) / n_freq)
    ang = jnp.concatenate([row[:, None] * inv, col[:, None] * inv], axis=-1)
    cos, sin = jnp.cos(ang), jnp.sin(ang)
    cos_h = jnp.concatenate([cos, cos], axis=-1)
    sin_h = jnp.concatenate([-sin, sin], axis=-1)
    cos_h = jnp.concatenate([jnp.ones((ctx_len, HD), F32), cos_h], axis=0)
    sin_h = jnp.concatenate([jnp.zeros((ctx_len, HD), F32), sin_h], axis=0)
    return jnp.tile(cos_h, (1, 2)), jnp.tile(sin_h, (1, 2))


def kernel(x, c, ctx, c_ctx, w_mod, b_mod, norm1_g, norm2_g, w_in, w_out, hgrn_lb, hgrn_norm_g,
           pool_w, pool_scale, q_norm_g, k_norm_g, router_w, router_b, moe_w_gate, moe_w_up, moe_w_down):
    bsz, t, d = x.shape
    ctx_len = ctx.shape[1]
    depth = w_mod.shape[0]
    s = ctx_len + t
    assert ctx_len % TOKEN_TILE == 0 and s % (TOKEN_TILE * TOKEN_NSUB) == 0 and t % GRID_W == 0
    assert ctx_len % HGRN_CHUNK == 0 and ATTN_TQ == TOKEN_TILE

    perm = np.concatenate([np.arange(0, HD, 2), np.arange(1, HD, 2)])
    na = 5 * A_WIDTH
    q0 = na + 256
    nqk = (C_HEADS + C_KV_HEADS) * HD
    w_qk = w_in[:, :, q0:q0 + nqk].reshape(depth, d, nqk // HD, HD // 2, 2)
    w_qk = jnp.swapaxes(w_qk, -1, -2).reshape(depth, d, nqk)
    w_in_b = jnp.concatenate([w_in[:, :, :q0], w_qk, w_in[:, :, q0 + nqk:]], axis=-1).astype(BF16)
    w_out_b = w_out.astype(BF16)
    gq = jnp.tile(q_norm_g[:, perm], (1, C_HEADS)).reshape(depth, 1, C_HEADS * HD)
    gk = jnp.tile(k_norm_g[:, perm], (1, C_KV_HEADS)).reshape(depth, 1, C_KV_HEADS * HD)
    ng = jnp.tile(hgrn_norm_g, (1, A_HEADS)).reshape(depth, 1, A_WIDTH)
    eye = jnp.eye(B_GROUPS, dtype=F32)
    pw_bd = (eye[None, :, None, :, None] * pool_w[:, :, :, None, :]).reshape(depth, 256, 256).astype(BF16)
    sm = jax.nn.softmax(hgrn_lb.astype(F32), axis=0)
    lower = jnp.cumsum(sm, axis=0) - sm[0:1]
    cos_t, sin_t = _rope_tables(t, ctx_len)
    bdq, bdk, bda = _block_diag_ones(C_HEADS * HD), _block_diag_ones(C_KV_HEADS * HD), _block_diag_ones(A_WIDTH)
    rw_t = router_w.T
    rb = router_b.reshape(N_EXPERTS, 1).astype(F32)

    cond = jnp.zeros((8, d), F32).at[:bsz].set(c).at[bsz].set(c_ctx)
    mods = _mod_call(cond, w_mod, b_mod)
    mods = mods[:, :bsz + 1].reshape(depth, bsz + 1, 6, d)
    mods = jnp.pad(mods, ((0, 0), (0, 0), (0, 2), (0, 0)))

    xs = jnp.concatenate([ctx, x], axis=1)
    moe = None
    n = bsz * s
    wps = s // SC_WINDOW

    def src_block(w):
        j = w % wps
        bk = w // wps
        return (bk // (2 * bsz)) * (n // SC_WINDOW) + ((bk // 2) % bsz) * wps + j

    for l in range(depth):
        outs = _in_call(xs, moe, mods[l - 1:l] if l else None, mods[l:l + 1], norm1_g[l:l + 1], w_in_b, l,
                        cos_t, sin_t, gq[l], gk[l], bdq, bdk, ctx_len)
        if moe is not None:
            xs, outs = outs[0], outs[1:]
        hg, pool, qh, kh, vh, qn2, kn2 = outs
        of, ob = _hgrn_call(hg, lower[l], ctx_len)
        oc = _attention(qh, kh, vh, qn2, kn2, ctx_len)
        xs, h2c, idx, wcol = _out_call(xs, of, ob, hg, pool, oc, mods[l:l + 1], ng[l], pw_bd[l],
                                       pool_scale[l:l + 1], w_out_b, l, norm2_g[l:l + 1], rw_t, rb, bda, ctx_len)
        slot, block_expert, nblocks, p = _moe_plan(idx, n)
        rows = (jnp.arange(MOE_NC, dtype=jnp.int32)[:, None] * p + slot[None, :]).reshape(-1)
        xg = _sc_scatter_rows(h2c.reshape(MOE_NC * n, MOE_CW), rows, MOE_NC * p, src_block)
        ys = _moe_call(block_expert, nblocks, xg.reshape(MOE_NC, p, MOE_CW),
                       moe_w_gate, moe_w_up, moe_w_down, l)
        yc = _sc_gather_rows(ys.reshape(MOE_NC * p, MOE_CW), rows)
        moe = (yc.reshape(MOE_NC, bsz, 2, s, MOE_CW), wcol)
    return _final_call(xs, moe[0], moe[1], mods[depth - 1:depth], ctx_len)
```
